```python
import math
import jax, jax.numpy as jnp
from jax import lax
import numpy as np

D_MODEL = 1024
BATCH = 16
SEQ = 4096
DEPTH = 1

GDN_HEADS = 8
GDN_HEAD_DIM = 128
GDN_WIDTH = GDN_HEADS * GDN_HEAD_DIM
GDN_CONV = 4
GDN_CHUNK = 64
SC_WIDTH = D_MODEL
SC_CONV = 3
N_EXPERTS = 32
TOP_K = 4
D_FF = D_MODEL
SWIGLU_LIMIT = 7.0
SWIGLU_ALPHA = 1.702
MOE_BLOCK = 128
NORM_EPS = 1e-6
IN_SPLITS = (3 * GDN_WIDTH, GDN_WIDTH, GDN_HEADS, GDN_HEADS, SC_WIDTH, SC_WIDTH, SC_WIDTH, D_MODEL, D_MODEL)
D_IN = 4 * GDN_WIDTH + 2 * GDN_HEADS + 3 * SC_WIDTH + 2 * D_MODEL

kernel_name = 'hybrid_gdn_shortconv_moe_block'


def rms_norm(x, g):
    xf = x.astype(jnp.float32)
    y = xf * lax.rsqrt(jnp.mean(xf * xf, axis=-1, keepdims=True) + NORM_EPS)
    return (y * g.astype(jnp.float32)).astype(x.dtype)


def l2_normalize(x):
    xf = x.astype(jnp.float32)
    return xf * lax.rsqrt(jnp.sum(xf * xf, axis=-1, keepdims=True) + NORM_EPS)


def causal_depthwise_conv(x, w):
    K = w.shape[0]
    S = x.shape[1]
    xp = jnp.pad(x, ((0, 0), (K - 1, 0), (0, 0)))
    y = w[0] * xp[:, 0:S]
    for j in range(1, K):
        y = y + w[j] * xp[:, j:j + S]
    return y


def gated_delta_rule_chunked(q, k, v, g, beta):
    Bn, S, H, Dk = q.shape
    Dv = v.shape[-1]
    C = GDN_CHUNK
    N = S // C

    def chunks(t):
        t = t.reshape((Bn, N, C, H) + t.shape[3:])
        return jnp.moveaxis(t, (1, 3), (0, 2))

    q = chunks(q) * (Dk ** -0.5)
    k = chunks(k)
    v = chunks(v)
    g = chunks(g)
    beta = chunks(beta)
    gcum = jnp.cumsum(g, axis=-1)
    idx = jnp.arange(C)
    causal = idx[:, None] >= idx[None, :]
    strict = idx[:, None] > idx[None, :]
    diff = gcum[..., :, None] - gcum[..., None, :]
    decay = jnp.where(causal, jnp.exp(jnp.where(causal, diff, 0.0)), 0.0)
    kb = k * beta[..., None]
    Lm = jnp.where(strict, jnp.einsum('nbhid,nbhjd->nbhij', kb, k) * decay, 0.0)
    eye = jnp.eye(C, dtype=Lm.dtype)
    T = lax.linalg.triangular_solve(Lm + eye, jnp.broadcast_to(eye, Lm.shape),
                                    left_side=True, lower=True, unit_diagonal=True)
    u = T @ (v * beta[..., None])
    w = T @ (kb * jnp.exp(gcum)[..., None])
    aqk = jnp.einsum('nbhid,nbhjd->nbhij', q, k) * decay
    qg = q * jnp.exp(gcum)[..., None]
    g_last = gcum[..., -1]
    kd = k * jnp.exp(g_last[..., None] - gcum)[..., None]

    def step(state, inp):
        w_n, u_n, qg_n, aqk_n, kd_n, gl_n = inp
        v_new = u_n - w_n @ state
        o_n = qg_n @ state + aqk_n @ v_new
        state = state * jnp.exp(gl_n)[..., None, None] + jnp.einsum('bhck,bhcv->bhkv', kd_n, v_new)
        return state, o_n

    state0 = jnp.zeros((Bn, H, Dk, Dv), jnp.float32)
    _, o = lax.scan(step, state0, (w, u, qg, aqk, kd, g_last))
    return jnp.moveaxis(o, (0, 2), (1, 3)).reshape(Bn, S, H, Dv)


def hybrid_mixer(xn, w_in, gdn_conv_w, gdn_A_log, gdn_dt_bias, gdn_norm_g, w_o_gdn,
                 sconv_w, w_o_sconv, w_mix_out):
    Bn, S, _ = xn.shape
    proj = xn @ w_in
    offsets = np.cumsum(IN_SPLITS)[:-1].tolist()
    qkv, z, a, b, xs, gate_b, gate_c, gate_A, gate_B = jnp.split(proj, offsets, axis=-1)

    qkv = jax.nn.silu(causal_depthwise_conv(qkv, gdn_conv_w))
    q, k, v = jnp.split(qkv, 3, axis=-1)
    q = l2_normalize(q.reshape(Bn, S, GDN_HEADS, GDN_HEAD_DIM))
    k = l2_normalize(k.reshape(Bn, S, GDN_HEADS, GDN_HEAD_DIM))
    v = v.reshape(Bn, S, GDN_HEADS, GDN_HEAD_DIM).astype(jnp.float32)
    beta = jax.nn.sigmoid(b.astype(jnp.float32))
    g = -jnp.exp(gdn_A_log.astype(jnp.float32)) * jax.nn.softplus(
        a.astype(jnp.float32) + gdn_dt_bias.astype(jnp.float32))
    o = gated_delta_rule_chunked(q, k, v, g, beta).astype(xn.dtype)
    o = rms_norm(o, gdn_norm_g) * jax.nn.silu(z.reshape(Bn, S, GDN_HEADS, GDN_HEAD_DIM))
    y_a = o.reshape(Bn, S, GDN_WIDTH) @ w_o_gdn

    y_b = (gate_b * causal_depthwise_conv(gate_c * xs, sconv_w)) @ w_o_sconv

    merged = jax.nn.sigmoid(gate_A) * y_a + jax.nn.sigmoid(gate_B) * y_b
    return merged @ w_mix_out


def moe_ffn(xn, w_router, b_router, w_gate_up, b_gate_up, w_down, b_down):
    Bn, S, D = xn.shape
    Tn = Bn * S
    TK = Tn * TOP_K
    xf = xn.reshape(Tn, D)
    logits = xf.astype(jnp.float32) @ w_router.astype(jnp.float32) + b_router.astype(jnp.float32)
    top_vals, top_idx = lax.top_k(logits, TOP_K)
    top_w = jax.nn.softmax(top_vals, axis=-1).astype(xn.dtype)

    e_flat = top_idx.reshape(-1)
    tok_flat = jnp.arange(TK, dtype=jnp.int32) // TOP_K
    order = jnp.argsort(e_flat)
    e_sorted = e_flat[order]
    tok_sorted = tok_flat[order]
    w_sorted = top_w.reshape(-1)[order]
    counts = jnp.bincount(e_flat, length=N_EXPERTS)
    start = jnp.cumsum(counts) - counts
    pcounts = ((counts + MOE_BLOCK - 1) // MOE_BLOCK) * MOE_BLOCK
    pend = jnp.cumsum(pcounts)
    pstart = pend - pcounts
    dest = pstart[e_sorted] + (jnp.arange(TK, dtype=jnp.int32) - start[e_sorted])
    n_blocks = (TK + N_EXPERTS * (MOE_BLOCK - 1) + MOE_BLOCK - 1) // MOE_BLOCK
    x_pad = jnp.zeros((n_blocks * MOE_BLOCK, D), xn.dtype).at[dest].set(xf[tok_sorted])
    block_e = jnp.minimum(jnp.searchsorted(pend, jnp.arange(n_blocks) * MOE_BLOCK, side='right'),
                          N_EXPERTS - 1)

    def expert_block(args):
        xb, e = args
        gu = xb @ w_gate_up[e] + b_gate_up[e]
        gate = jnp.minimum(gu[:, :D_FF], SWIGLU_LIMIT)
        up = jnp.clip(gu[:, D_FF:], -SWIGLU_LIMIT, SWIGLU_LIMIT)
        hid = (up + 1.0) * gate * jax.nn.sigmoid(SWIGLU_ALPHA * gate)
        return hid @ w_down[e] + b_down[e]

    y_blocks = lax.map(expert_block, (x_pad.reshape(n_blocks, MOE_BLOCK, D), block_e))
    y_sorted = y_blocks.reshape(n_blocks * MOE_BLOCK, D)[dest]
    out = jnp.zeros((Tn, D), xn.dtype).at[tok_sorted].add(y_sorted * w_sorted[:, None])
    return out.reshape(Bn, S, D)


def setup_inputs(seed: int = 0) -> dict:
    key = jax.random.key(seed)
    ks = jax.random.split(key, 20)
    f32 = jnp.float32
    L = DEPTH

    def nrm(k, shape, fan_in):
        return jax.random.normal(k, shape, f32) * (fan_in ** -0.5)

    def gain(k, shape):
        return 1.0 + 0.02 * jax.random.normal(k, shape, f32)

    def small(k, shape):
        return 0.01 * jax.random.normal(k, shape, f32)

    dt = jnp.exp(jax.random.uniform(ks[6], (L, GDN_HEADS), f32, math.log(1e-3), math.log(1e-1)))
    return {
        'x': jax.random.normal(ks[0], (BATCH, SEQ, D_MODEL), f32),
        'norm_mix_g': gain(ks[1], (L, D_MODEL)),
        'w_in': nrm(ks[2], (L, D_MODEL, D_IN), D_MODEL),
        'gdn_conv_w': nrm(ks[3], (L, GDN_CONV, 3 * GDN_WIDTH), GDN_CONV),
        'gdn_A_log': jnp.log(jax.random.uniform(ks[4], (L, GDN_HEADS), f32, 1.0, 16.0)),
        'gdn_dt_bias': dt + jnp.log(-jnp.expm1(-dt)),
        'gdn_norm_g': gain(ks[5], (L, GDN_HEAD_DIM)),
        'w_o_gdn': nrm(ks[7], (L, GDN_WIDTH, D_MODEL), GDN_WIDTH),
        'sconv_w': nrm(ks[8], (L, SC_CONV, SC_WIDTH), SC_CONV),
        'w_o_sconv': nrm(ks[9], (L, SC_WIDTH, D_MODEL), SC_WIDTH),
        'w_mix_out': nrm(ks[10], (L, D_MODEL, D_MODEL), D_MODEL),
        'norm_ffn_g': gain(ks[11], (L, D_MODEL)),
        'w_router': nrm(ks[12], (L, D_MODEL, N_EXPERTS), D_MODEL),
        'b_router': small(ks[13], (L, N_EXPERTS)),
        'w_gate_up': nrm(ks[14], (L, N_EXPERTS, D_MODEL, 2 * D_FF), D_MODEL),
        'b_gate_up': small(ks[15], (L, N_EXPERTS, 2 * D_FF)),
        'w_down': nrm(ks[16], (L, N_EXPERTS, D_FF, D_MODEL), D_FF),
        'b_down': small(ks[17], (L, N_EXPERTS, D_MODEL)),
        'norm_final_g': gain(ks[18], (D_MODEL,)),
    }


def reference(x, norm_mix_g, w_in, gdn_conv_w, gdn_A_log, gdn_dt_bias, gdn_norm_g, w_o_gdn,
              sconv_w, w_o_sconv, w_mix_out, norm_ffn_g, w_router, b_router, w_gate_up,
              b_gate_up, w_down, b_down, norm_final_g):
    h = x
    for l in range(DEPTH):
        h = h + hybrid_mixer(rms_norm(h, norm_mix_g[l]), w_in[l], gdn_conv_w[l], gdn_A_log[l],
                             gdn_dt_bias[l], gdn_norm_g[l], w_o_gdn[l], sconv_w[l], w_o_sconv[l],
                             w_mix_out[l])
        h = h + moe_ffn(rms_norm(h, norm_ffn_g[l]), w_router[l], b_router[l], w_gate_up[l],
                        b_gate_up[l], w_down[l], b_down[l])
    return rms_norm(h, norm_final_g)
```

```python
import functools

import jax
import jax.numpy as jnp
from jax import lax
from jax.experimental import pallas as pl
from jax.experimental.pallas import tpu as pltpu

F32 = jnp.float32
BF16 = jnp.bfloat16
I32 = jnp.int32

D_MODEL = 1024
HEADS = 8
HEAD_DIM = 128
GDN_WIDTH = HEADS * HEAD_DIM
GDN_CONV = 4
CHUNK = 64
SC_CONV = 3
N_EXPERTS = 32
TOP_K = 4
D_FF = 1024
SWIGLU_LIMIT = 7.0
SWIGLU_ALPHA = 1.702
NORM_EPS = 1e-6

LANES = 128
SUBLANES = 8
VMEM_LIMIT = 56 * 1024 * 1024

TM_IN = 512
CT_IN = 256
TS_GDN = 512
TM_MIX = 512
TM_DISP = 512
BM_MOE = 512
TM_COMB = 256


def _sigmoid(x):
    return 1.0 / (1.0 + jnp.exp(-x))


def _silu(x):
    return x * _sigmoid(x)


def _softplus(x):
    return jnp.maximum(x, 0.0) + jnp.log(1.0 + jnp.exp(-jnp.abs(x)))


def _rms_norm(x, g):
    return x * lax.rsqrt(jnp.mean(x * x, axis=-1, keepdims=True) + NORM_EPS) * g


def _dot(a, b):
    return jnp.dot(a, b, preferred_element_type=F32)


def _dot_nt(a, b):
    return lax.dot_general(a, b, (((1,), (1,)), ((), ())), preferred_element_type=F32)


def _dot_tn(a, b):
    return lax.dot_general(a, b, (((0,), (0,)), ((), ())), preferred_element_type=F32)


ROW_TILES = D_MODEL // LANES


def _store_row_tiles(ref, val):
    m = val.shape[0]
    for s in range(ROW_TILES):
        ref[pl.ds(s, m, stride=ROW_TILES), :] = val[:, s * LANES:(s + 1) * LANES]


def _load_row_tiles(ref, m, s):
    return ref[pl.ds(s, m, stride=ROW_TILES), :]


def _row_tile(ref, row):
    return ref.at[pl.ds(pl.multiple_of(row * ROW_TILES, ROW_TILES), ROW_TILES), :]


def _causal_conv(p, prev, w):
    taps = w.shape[0]
    row8 = lax.broadcasted_iota(I32, (SUBLANES, p.shape[1]), 0)
    p8 = p[0:SUBLANES]
    y = w[taps - 1:taps] * p
    yh = w[taps - 1:taps] * p8
    for s in range(1, taps):
        wj = w[taps - 1 - s:taps - s]
        y = y + wj * pltpu.roll(p, s, 0)
        hs = jnp.where(row8 < s, pltpu.roll(prev, s, 0), pltpu.roll(p8, s, 0))
        yh = yh + wj * hs
    return jnp.concatenate([yh, y[SUBLANES:]], axis=0)


def _inproj_kernel(x_ref, ng_ref, wqkv_ref, wz_ref, wab_ref, wxs_ref, wgb_ref, wgc_ref, wga_ref, wgt_ref,
                   cw_ref, scw_ref, alog_ref, dtb_ref,
                   q_ref, k_ref, v_ref, zs_ref, gbeta_ref, sb_ref, ga_ref, gt_ref,
                   carry_qkv, carry_u, xn_scr):
    tm = x_ref.shape[0]

    @pl.when(pl.program_id(1) == 0)
    def _():
        carry_qkv[...] = jnp.zeros_like(carry_qkv)
        carry_u[...] = jnp.zeros_like(carry_u)

    x = x_ref[...]
    xn_scr[...] = _rms_norm(x, ng_ref[...]).astype(BF16)

    q_scale = HEAD_DIM ** -0.5
    outs = (q_ref, k_ref, v_ref)
    for ci in range(3 * GDN_WIDTH // CT_IN):
        c0 = ci * CT_IN
        p = _dot(xn_scr[...], wqkv_ref[:, c0:c0 + CT_IN])
        y = _causal_conv(p, carry_qkv[:, c0:c0 + CT_IN], cw_ref[:, c0:c0 + CT_IN])
        carry_qkv[:, c0:c0 + CT_IN] = p[tm - SUBLANES:tm]
        y = _silu(y)
        which = c0 // GDN_WIDTH
        o0 = c0 - which * GDN_WIDTH
        for hh in range(CT_IN // HEAD_DIM):
            yh = y[:, hh * HEAD_DIM:(hh + 1) * HEAD_DIM]
            if which < 2:
                yh = yh * lax.rsqrt(jnp.sum(yh * yh, axis=-1, keepdims=True) + NORM_EPS)
            if which == 0:
                yh = yh * q_scale
            outs[which][:, o0 + hh * HEAD_DIM:o0 + (hh + 1) * HEAD_DIM] = yh.astype(BF16)

    for ci in range(D_MODEL // CT_IN):
        c0 = ci * CT_IN
        zs_ref[:, c0:c0 + CT_IN] = _silu(_dot(xn_scr[...], wz_ref[:, c0:c0 + CT_IN])).astype(BF16)
        ga_ref[:, c0:c0 + CT_IN] = _sigmoid(_dot(xn_scr[...], wga_ref[:, c0:c0 + CT_IN])).astype(BF16)
        gt_ref[:, c0:c0 + CT_IN] = _sigmoid(_dot(xn_scr[...], wgt_ref[:, c0:c0 + CT_IN])).astype(BF16)

    for ci in range(D_MODEL // CT_IN):
        c0 = ci * CT_IN
        xs = _dot(xn_scr[...], wxs_ref[:, c0:c0 + CT_IN])
        gc = _dot(xn_scr[...], wgc_ref[:, c0:c0 + CT_IN])
        u = gc * xs
        y = _causal_conv(u, carry_u[:, c0:c0 + CT_IN], scw_ref[:, c0:c0 + CT_IN])
        carry_u[:, c0:c0 + CT_IN] = u[tm - SUBLANES:tm]
        gb = _dot(xn_scr[...], wgb_ref[:, c0:c0 + CT_IN])
        sb_ref[:, c0:c0 + CT_IN] = (gb * y).astype(BF16)

    ab = _dot(xn_scr[...], wab_ref[...])
    lane = lax.broadcasted_iota(I32, ab.shape, 1)
    g = -jnp.exp(alog_ref[...]) * _softplus(ab + dtb_ref[...])
    gbeta_ref[...] = jnp.where(lane < HEADS, g, _sigmoid(ab))


def _inproj(x, norm_g, w_in, conv_w, a_log, dt_bias, sconv_w):
    bsz, seq, _ = x.shape
    tm = min(TM_IN, seq)
    o = 0
    w = {}
    for name, width in (("qkv", 3 * GDN_WIDTH), ("z", GDN_WIDTH), ("a", HEADS), ("b", HEADS),
                        ("xs", D_MODEL), ("gb", D_MODEL), ("gc", D_MODEL), ("ga", D_MODEL), ("gt", D_MODEL)):
        w[name] = w_in[:, o:o + width]
        o += width
    wab = jnp.zeros((D_MODEL, LANES), F32).at[:, :HEADS].set(w["a"]).at[:, HEADS:2 * HEADS].set(w["b"])
    alog = jnp.zeros((1, LANES), F32).at[0, :HEADS].set(a_log)
    dtb = jnp.zeros((1, LANES), F32).at[0, :HEADS].set(dt_bias)
    weights = [w["qkv"], w["z"], wab, w["xs"], w["gb"], w["gc"], w["ga"], w["gt"]]
    weights = [a.astype(BF16) for a in weights]

    def const(shape):
        return pl.BlockSpec(shape, lambda b, s: (0,) * len(shape), pipeline_mode=pl.Buffered(1))

    def tok(width):
        return pl.BlockSpec((None, tm, width), lambda b, s: (b, s, 0))

    out_bf = jax.ShapeDtypeStruct((bsz, seq, D_MODEL), BF16)
    return pl.pallas_call(
        _inproj_kernel,
        grid=(bsz, seq // tm),
        in_specs=[tok(D_MODEL), const((1, D_MODEL))] + [const(a.shape) for a in weights]
        + [const((GDN_CONV, 3 * GDN_WIDTH)), const((SC_CONV, D_MODEL)), const((1, LANES)), const((1, LANES))],
        out_specs=[tok(D_MODEL)] * 4 + [tok(LANES)] + [tok(D_MODEL)] * 3,
        out_shape=[out_bf] * 4 + [jax.ShapeDtypeStruct((bsz, seq, LANES), F32)] + [out_bf] * 3,
        scratch_shapes=[pltpu.VMEM((SUBLANES, 3 * GDN_WIDTH), F32), pltpu.VMEM((SUBLANES, D_MODEL), F32),
                        pltpu.VMEM((tm, D_MODEL), BF16)],
        compiler_params=pltpu.CompilerParams(dimension_semantics=("arbitrary", "arbitrary"),
                                             vmem_limit_bytes=VMEM_LIMIT),
        name="inproj",
    )(x, norm_g.reshape(1, D_MODEL), *weights, conv_w, sconv_w, alog, dtb)


def _gdn_kernel(q_ref, k_ref, v_ref, zs_ref, gbeta_ref, ng_ref, o_ref, state_ref):
    ts = q_ref.shape[0]

    @pl.when(pl.program_id(1) == 0)
    def _():
        state_ref[...] = jnp.zeros_like(state_ref)

    ri = lax.broadcasted_iota(I32, (CHUNK, CHUNK), 0)
    ci = lax.broadcasted_iota(I32, (CHUNK, CHUNK), 1)
    causal = ri >= ci
    strict = ri > ci
    ltri = causal.astype(F32)
    ng = ng_ref[...]

    def chunk(c, carry):
        rows = pl.ds(pl.multiple_of(c * CHUNK, CHUNK), CHUNK)
        gbc = gbeta_ref[rows, :]
        gcum = jnp.dot(ltri, gbc, precision=lax.Precision.HIGHEST, preferred_element_type=F32)
        gcum_t = gcum.T
        for h in range(HEADS):
            cs = slice(h * HEAD_DIM, (h + 1) * HEAD_DIM)
            qh = q_ref[rows, cs]
            kh = k_ref[rows, cs]
            kf = kh.astype(F32)
            vf = v_ref[rows, cs].astype(F32)
            gc = gcum[:, h:h + 1]
            gr = gcum_t[h:h + 1, :]
            beta = gbc[:, HEADS + h:HEADS + h + 1]
            decay = jnp.where(causal, jnp.exp(jnp.where(causal, gc - gr, 0.0)), 0.0)
            eg = jnp.exp(gc)
            g_last = gc[CHUNK - 1:CHUNK, :]
            kb = kf * beta
            lm = jnp.where(strict, _dot_nt(kb.astype(BF16), kh) * decay, 0.0)
            xs = jnp.concatenate([vf * beta, kb * eg], axis=1)
            xs = xs - _dot(lm.astype(BF16), xs.astype(BF16))
            pw = lm
            for _ in range(5):
                pwb = pw.astype(BF16)
                pw = _dot(pwb, pwb)
                xs = xs + _dot(pw.astype(BF16), xs.astype(BF16))
            u = xs[:, :HEAD_DIM]
            w = xs[:, HEAD_DIM:]
            aqk = _dot_nt(qh, kh) * decay
            qg = qh.astype(F32) * eg
            kd = kf * jnp.exp(g_last - gc)
            st = state_ref[h]
            stb = st.astype(BF16)
            v_new = u - _dot(w.astype(BF16), stb)
            vnb = v_new.astype(BF16)
            o = _dot(qg.astype(BF16), stb) + _dot(aqk.astype(BF16), vnb)
            state_ref[h] = st * jnp.exp(g_last) + _dot_tn(kd.astype(BF16), vnb)
            o = _rms_norm(o, ng) * zs_ref[rows, cs].astype(F32)
            o_ref[rows, cs] = o.astype(BF16)
        return carry

    lax.fori_loop(0, ts // CHUNK, chunk, 0)


def _gdn(q, k, v, zs, gbeta, norm_g):
    bsz, seq, _ = q.shape
    ts = min(TS_GDN, seq)

    def tok(width):
        return pl.BlockSpec((None, ts, width), lambda b, s: (b, s, 0))

    return pl.pallas_call(
        _gdn_kernel,
        grid=(bsz, seq // ts),
        in_specs=[tok(GDN_WIDTH)] * 4 + [tok(LANES), pl.BlockSpec((1, HEAD_DIM), lambda b, s: (0, 0))],
        out_specs=tok(GDN_WIDTH),
        out_shape=jax.ShapeDtypeStruct((bsz, seq, GDN_WIDTH), BF16),
        scratch_shapes=[pltpu.VMEM((HEADS, HEAD_DIM, HEAD_DIM), F32)],
        compiler_params=pltpu.CompilerParams(dimension_semantics=("arbitrary", "arbitrary"),
                                             vmem_limit_bytes=VMEM_LIMIT),
        name="gdn",
    )(q, k, v, zs, gbeta, norm_g.reshape(1, HEAD_DIM))


def _mix_kernel(og_ref, sb_ref, ga_ref, gt_ref, x_ref, wog_ref, wos_ref, wmix_ref, gffn_ref, wr_ref, br_ref,
                h1_ref, xn2_ref, idx_ref, rank_ref, tw_ref, cnt_ref, carry_ref):
    tm = x_ref.shape[0]

    @pl.when(pl.program_id(0) == 0)
    def _():
        carry_ref[...] = jnp.zeros_like(carry_ref)

    ya = _dot(og_ref[...], wog_ref[...])
    yb = _dot(sb_ref[...], wos_ref[...])
    merged = ga_ref[...].astype(F32) * ya + gt_ref[...].astype(F32) * yb
    h1 = x_ref[...] + _dot(merged.astype(BF16), wmix_ref[...])
    h1_ref[...] = h1
    xn2 = _rms_norm(h1, gffn_ref[...])
    _store_row_tiles(xn2_ref, xn2)

    logits = jnp.dot(xn2, wr_ref[...], precision=lax.Precision.HIGHEST, preferred_element_type=F32) + br_ref[...]
    lane = lax.broadcasted_iota(I32, (tm, LANES), 1)
    lane_f = lane.astype(F32)
    neg_inf = jnp.float32(-jnp.inf)
    work = jnp.where(lane < N_EXPERTS, logits, neg_inf)
    hits, vals = [], []
    for _ in range(TOP_K):
        m = jnp.max(work, axis=-1, keepdims=True)
        first = jnp.min(jnp.where(work == m, lane_f, float(LANES)), axis=-1, keepdims=True)
        hit = lane_f == first
        work = jnp.where(hit, neg_inf, work)
        hits.append(hit)
        vals.append((m, first))
    exps = [jnp.exp(m - vals[0][0]) for m, _ in vals]
    denom = exps[0] + exps[1] + exps[2] + exps[3]

    sel = jnp.zeros((tm, LANES), F32)
    for hit in hits:
        sel = sel + hit.astype(F32)
    ri = lax.broadcasted_iota(I32, (tm, tm), 0)
    ci = lax.broadcasted_iota(I32, (tm, tm), 1)
    before = (ri > ci).astype(BF16)
    rank_mat = _dot(before, sel.astype(BF16)) + carry_ref[...]
    carry_ref[...] = carry_ref[...] + jnp.sum(sel, axis=0, keepdims=True)
    cnt_ref[...] = carry_ref[...]

    idx_out = jnp.zeros((tm, LANES), F32)
    rank_out = jnp.zeros((tm, LANES), F32)
    tw_out = jnp.zeros((tm, LANES), F32)
    for kk in range(TOP_K):
        slot = lane == kk
        rk = jnp.sum(jnp.where(hits[kk], rank_mat, 0.0), axis=-1, keepdims=True)
        idx_out = jnp.where(slot, vals[kk][1], idx_out)
        rank_out = jnp.where(slot, rk, rank_out)
        tw_out = jnp.where(slot, exps[kk] / denom, tw_out)
    idx_ref[...] = idx_out.astype(I32)
    rank_ref[...] = rank_out.astype(I32)
    tw_ref[...] = tw_out


def _mix(og, sb, ga, gt, x, w_o_gdn, w_o_sconv, w_mix_out, norm_ffn_g, w_router, b_router):
    t = x.shape[0]
    tm = min(TM_MIX, t)
    wr = jnp.zeros((D_MODEL, LANES), F32).at[:, :N_EXPERTS].set(w_router)
    br = jnp.zeros((1, LANES), F32).at[0, :N_EXPERTS].set(b_router)

    def const(shape):
        return pl.BlockSpec(shape, lambda i: (0,) * len(shape), pipeline_mode=pl.Buffered(1))

    def tok(width):
        return pl.BlockSpec((tm, width), lambda i: (i, 0))

    return pl.pallas_call(
        _mix_kernel,
        grid=(t // tm,),
        in_specs=[tok(D_MODEL)] * 5 + [const((D_MODEL, D_MODEL))] * 3
        + [const((1, D_MODEL)), const((D_MODEL, LANES)), const((1, LANES))],
        out_specs=[tok(D_MODEL), pl.BlockSpec((tm * ROW_TILES, LANES), lambda i: (i, 0)),
                   tok(LANES), tok(LANES), tok(LANES), pl.BlockSpec((1, LANES), lambda i: (0, 0))],
        out_shape=[jax.ShapeDtypeStruct((t, D_MODEL), F32), jax.ShapeDtypeStruct((t * ROW_TILES, LANES), F32),
                   jax.ShapeDtypeStruct((t, LANES), I32), jax.ShapeDtypeStruct((t, LANES), I32),
                   jax.ShapeDtypeStruct((t, LANES), F32), jax.ShapeDtypeStruct((1, LANES), F32)],
        scratch_shapes=[pltpu.VMEM((1, LANES), F32)],
        compiler_params=pltpu.CompilerParams(dimension_semantics=("arbitrary",), vmem_limit_bytes=VMEM_LIMIT),
        name="mix_router",
    )(og, sb, ga, gt, x, w_o_gdn.astype(BF16), w_o_sconv.astype(BF16), w_mix_out.astype(BF16),
      norm_ffn_g.reshape(1, D_MODEL), wr, br)


def _dispatch_kernel(last_ref, has_ref, dest_ref, x_ref, xpad_ref, zero_ref, sem_z, sem):
    bm = zero_ref.shape[0] // ROW_TILES
    n = dest_ref.shape[2]
    tok0 = pl.program_id(0) * (n // TOP_K)

    @pl.when(pl.program_id(0) == 0)
    def _():
        zero_ref[...] = jnp.zeros_like(zero_ref)
        for e in range(N_EXPERTS):
            @pl.when(has_ref[e] > 0)
            def _():
                start = pl.multiple_of(last_ref[e] * ROW_TILES, bm * ROW_TILES)
                cp = pltpu.make_async_copy(zero_ref, xpad_ref.at[pl.ds(start, bm * ROW_TILES), :], sem_z)
                cp.start()
                cp.wait()

    def row_copy(j):
        return pltpu.make_async_copy(_row_tile(x_ref, tok0 + j // TOP_K),
                                     _row_tile(xpad_ref, dest_ref[0, 0, j]), sem)

    def issue(j, carry):
        row_copy(j).start()
        return carry

    def drain(j, carry):
        row_copy(j).wait()
        return carry

    lax.fori_loop(0, n, issue, 0)
    lax.fori_loop(0, n, drain, 0)


def _dispatch(xn2_tiles, dest, last_blk, has_blk, n_slots, bm):
    t = xn2_tiles.shape[0] // ROW_TILES
    tm = min(TM_DISP, t)
    nt = t // tm
    grid_spec = pltpu.PrefetchScalarGridSpec(
        num_scalar_prefetch=2,
        grid=(nt,),
        in_specs=[pl.BlockSpec((1, 1, tm * TOP_K), lambda i, a, b: (i, 0, 0), memory_space=pltpu.SMEM),
                  pl.BlockSpec(memory_space=pl.ANY)],
        out_specs=pl.BlockSpec(memory_space=pl.ANY),
        scratch_shapes=[pltpu.VMEM((bm * ROW_TILES, LANES), F32), pltpu.SemaphoreType.DMA(()),
                        pltpu.SemaphoreType.DMA(())],
    )
    return pl.pallas_call(
        _dispatch_kernel,
        grid_spec=grid_spec,
        out_shape=jax.ShapeDtypeStruct((n_slots * ROW_TILES, LANES), F32),
        compiler_params=pltpu.CompilerParams(dimension_semantics=("arbitrary",), vmem_limit_bytes=VMEM_LIMIT,
                                             has_side_effects=True),
        name="dispatch",
    )(last_blk, has_blk, dest.reshape(nt, 1, tm * TOP_K), xn2_tiles)


def _moe_kernel(be_ref, nu_ref, x_ref, wgu_ref, bgu_ref, wd_ref, bd_ref, y_ref):
    bm = x_ref.shape[0] // ROW_TILES

    @pl.when(pl.program_id(0) < nu_ref[0])
    def _():
        xb = jnp.concatenate([_load_row_tiles(x_ref, bm, s).astype(BF16) for s in range(ROW_TILES)], axis=1)
        gu = _dot(xb, wgu_ref[...]) + bgu_ref[...]
        gate = jnp.minimum(gu[:, :D_FF], SWIGLU_LIMIT)
        up = jnp.clip(gu[:, D_FF:], -SWIGLU_LIMIT, SWIGLU_LIMIT)
        hid = (up + 1.0) * gate * _sigmoid(SWIGLU_ALPHA * gate)
        _store_row_tiles(y_ref, _dot(hid.astype(BF16), wd_ref[...]) + bd_ref[...])


def _moe(xpad, block_e, n_used, w_gate_up, b_gate_up, w_down, b_down, bm):
    nb = xpad.shape[0] // (bm * ROW_TILES)

    def rows(i, be, nu):
        return (jnp.minimum(i, nu[0] - 1), 0)

    def expert(i, be, nu):
        return (be[i], 0, 0)

    grid_spec = pltpu.PrefetchScalarGridSpec(
        num_scalar_prefetch=2,
        grid=(nb,),
        in_specs=[pl.BlockSpec((bm * ROW_TILES, LANES), rows),
                  pl.BlockSpec((None, D_MODEL, 2 * D_FF), expert),
                  pl.BlockSpec((None, 1, 2 * D_FF), expert),
                  pl.BlockSpec((None, D_FF, D_MODEL), expert),
                  pl.BlockSpec((None, 1, D_MODEL), expert)],
        out_specs=pl.BlockSpec((bm * ROW_TILES, LANES), rows),
    )
    return pl.pallas_call(
        _moe_kernel,
        grid_spec=grid_spec,
        out_shape=jax.ShapeDtypeStruct(xpad.shape, F32),
        compiler_params=pltpu.CompilerParams(dimension_semantics=("arbitrary",), vmem_limit_bytes=VMEM_LIMIT),
        name="moe_mlp",
    )(block_e, n_used, xpad, w_gate_up.astype(BF16), b_gate_up.reshape(N_EXPERTS, 1, 2 * D_FF),
      w_down.astype(BF16), b_down.reshape(N_EXPERTS, 1, D_MODEL))


def _combine_kernel(dest_ref, h1_ref, tw_ref, gf_ref, ypad_ref, out_ref, buf_ref, sem):
    tm = h1_ref.shape[0]
    n = tm * TOP_K

    def row_copy(j):
        return pltpu.make_async_copy(_row_tile(ypad_ref, dest_ref[0, 0, j]),
                                     _row_tile(buf_ref.at[j % TOP_K], j // TOP_K), sem)

    def issue(j, carry):
        row_copy(j).start()
        return carry

    def drain(j, carry):
        row_copy(j).wait()
        return carry

    lax.fori_loop(0, n, issue, 0)
    lax.fori_loop(0, n, drain, 0)

    tw = tw_ref[...]
    accs = []
    ssq = jnp.zeros((tm, 1), F32)
    for s in range(ROW_TILES):
        acc = h1_ref[:, s * LANES:(s + 1) * LANES]
        for kk in range(TOP_K):
            acc = acc + tw[:, kk:kk + 1] * _load_row_tiles(buf_ref.at[kk], tm, s)
        ssq = ssq + jnp.sum(acc * acc, axis=-1, keepdims=True)
        accs.append(acc)
    inv = lax.rsqrt(ssq * (1.0 / D_MODEL) + NORM_EPS)
    for s in range(ROW_TILES):
        out_ref[:, s * LANES:(s + 1) * LANES] = accs[s] * inv * gf_ref[:, s * LANES:(s + 1) * LANES]


def _combine(ypad, dest, h1, tw, norm_final_g):
    t = h1.shape[0]
    tm = min(TM_COMB, t)
    nt = t // tm
    return pl.pallas_call(
        _combine_kernel,
        grid=(nt,),
        in_specs=[pl.BlockSpec((1, 1, tm * TOP_K), lambda i: (i, 0, 0), memory_space=pltpu.SMEM),
                  pl.BlockSpec((tm, D_MODEL), lambda i: (i, 0)),
                  pl.BlockSpec((tm, LANES), lambda i: (i, 0)),
                  pl.BlockSpec((1, D_MODEL), lambda i: (0, 0)),
                  pl.BlockSpec(memory_space=pl.ANY)],
        out_specs=pl.BlockSpec((tm, D_MODEL), lambda i: (i, 0)),
        out_shape=jax.ShapeDtypeStruct((t, D_MODEL), F32),
        scratch_shapes=[pltpu.VMEM((TOP_K, tm * ROW_TILES, LANES), F32), pltpu.SemaphoreType.DMA(())],
        compiler_params=pltpu.CompilerParams(dimension_semantics=("arbitrary",), vmem_limit_bytes=VMEM_LIMIT),
        name="combine",
    )(dest.reshape(nt, 1, tm * TOP_K), h1, tw, norm_final_g.reshape(1, D_MODEL), ypad)


def _routing_tables(idx_m, rank_m, cnt, t, bm):
    counts = cnt[0, :N_EXPERTS].astype(I32)
    pcounts = ((counts + bm - 1) // bm) * bm
    pend = jnp.cumsum(pcounts)
    pstart = pend - pcounts
    dest = pstart[idx_m[:, :TOP_K]] + rank_m[:, :TOP_K]
    n_slots = ((t * TOP_K + N_EXPERTS * (bm - 1) + bm - 1) // bm) * bm
    nb = n_slots // bm
    n_used = pend[-1] // bm
    block_e = jnp.minimum(jnp.searchsorted(pend, jnp.arange(nb, dtype=I32) * bm, side="right"), N_EXPERTS - 1)
    block_e = jnp.where(jnp.arange(nb) < n_used, block_e, block_e[jnp.maximum(n_used - 1, 0)]).astype(I32)
    last_blk = jnp.maximum(pend - bm, 0).astype(I32)
    has_blk = (pcounts > 0).astype(I32)
    return dest.astype(I32), block_e, n_used.reshape(1).astype(I32), last_blk, has_blk, n_slots


def kernel(x, norm_mix_g, w_in, gdn_conv_w, gdn_A_log, gdn_dt_bias, gdn_norm_g, w_o_gdn, sconv_w, w_o_sconv,
           w_mix_out, norm_ffn_g, w_router, b_router, w_gate_up, b_gate_up, w_down, b_down, norm_final_g):
    bsz, seq, _ = x.shape
    t = bsz * seq
    assert norm_mix_g.shape[0] == 1, "single-layer block only"
    q, k, v, zs, gbeta, sb, ga, gt = _inproj(x, norm_mix_g[0], w_in[0], gdn_conv_w[0], gdn_A_log[0],
                                             gdn_dt_bias[0], sconv_w[0])
    og = _gdn(q, k, v, zs, gbeta, gdn_norm_g[0])

    def flat(a):
        return a.reshape(t, a.shape[-1])

    h1, xn2, idx_m, rank_m, tw, cnt = _mix(flat(og), flat(sb), flat(ga), flat(gt), flat(x), w_o_gdn[0],
                                           w_o_sconv[0], w_mix_out[0], norm_ffn_g[0], w_router[0], b_router[0])
    bm = BM_MOE
    dest, block_e, n_used, last_blk, has_blk, n_slots = _routing_tables(idx_m, rank_m, cnt, t, bm)
    xpad = _dispatch(xn2, dest, last_blk, has_blk, n_slots, bm)
    ypad = _moe(xpad, block_e, n_used, w_gate_up[0], b_gate_up[0], w_down[0], b_down[0], bm)
    return _combine(ypad, dest, h1, tw, norm_final_g).reshape(bsz, seq, D_MODEL)
```

```python
import functools

import jax
import jax.numpy as jnp
from jax import lax
from jax.experimental import pallas as pl
from jax.experimental.pallas import tpu as pltpu

F32 = jnp.float32
BF16 = jnp.bfloat16
I32 = jnp.int32

D_MODEL = 1024
HEADS = 8
HEAD_DIM = 128
GDN_WIDTH = HEADS * HEAD_DIM
GDN_CONV = 4
CHUNK = 128
SC_CONV = 3
N_EXPERTS = 32
TOP_K = 4
D_FF = 1024
SWIGLU_LIMIT = 7.0
SWIGLU_ALPHA = 1.702
NORM_EPS = 1e-6

LANES = 128
SUBLANES = 8
VMEM_LIMIT = 56 * 1024 * 1024

TM_IN = 512
CT_IN = 256
TS_GDN = 512
TM_MIX = 512
TM_DISP = 512
BM_MOE = 512
TM_COMB = 256
ISSUE_UNROLL = 8


def _sigmoid(x):
    return 1.0 / (1.0 + jnp.exp(-x))


def _silu(x):
    return x * _sigmoid(x)


def _softplus(x):
    return jnp.maximum(x, 0.0) + jnp.log(1.0 + jnp.exp(-jnp.abs(x)))


def _rms_norm(x, g):
    return x * lax.rsqrt(jnp.mean(x * x, axis=-1, keepdims=True) + NORM_EPS) * g


def _dot(a, b):
    return jnp.dot(a, b, preferred_element_type=F32)


def _dot_nt(a, b):
    return lax.dot_general(a, b, (((1,), (1,)), ((), ())), preferred_element_type=F32)


def _dot_tn(a, b):
    return lax.dot_general(a, b, (((0,), (0,)), ((), ())), preferred_element_type=F32)


ROW_TILES = D_MODEL // LANES


def _store_row_tiles(ref, val):
    m = val.shape[0]
    for s in range(ROW_TILES):
        ref[pl.ds(s, m, stride=ROW_TILES), :] = val[:, s * LANES:(s + 1) * LANES]


def _load_row_tiles(ref, m, s):
    return ref[pl.ds(s, m, stride=ROW_TILES), :]


def _row_tile(ref, row):
    return ref.at[pl.ds(pl.multiple_of(row * ROW_TILES, ROW_TILES), ROW_TILES), :]


def _causal_conv(p, prev, w):
    taps = w.shape[0]
    row8 = lax.broadcasted_iota(I32, (SUBLANES, p.shape[1]), 0)
    p8 = p[0:SUBLANES]
    y = w[taps - 1:taps] * p
    yh = w[taps - 1:taps] * p8
    for s in range(1, taps):
        wj = w[taps - 1 - s:taps - s]
        y = y + wj * pltpu.roll(p, s, 0)
        hs = jnp.where(row8 < s, pltpu.roll(prev, s, 0), pltpu.roll(p8, s, 0))
        yh = yh + wj * hs
    return jnp.concatenate([yh, y[SUBLANES:]], axis=0)


def _inproj_kernel(x_ref, ng_ref, wqkv_ref, wz_ref, wab_ref, wxs_ref, wgb_ref, wgc_ref, wga_ref, wgt_ref,
                   cw_ref, scw_ref, alog_ref, dtb_ref,
                   q_ref, k_ref, v_ref, zs_ref, gbeta_ref, sb_ref, ga_ref, gt_ref,
                   carry_qkv, carry_u, xn_scr):
    tm = x_ref.shape[0]

    @pl.when(pl.program_id(1) == 0)
    def _():
        carry_qkv[...] = jnp.zeros_like(carry_qkv)
        carry_u[...] = jnp.zeros_like(carry_u)

    x = x_ref[...]
    xn_scr[...] = _rms_norm(x, ng_ref[...]).astype(BF16)

    q_scale = HEAD_DIM ** -0.5
    outs = (q_ref, k_ref, v_ref)
    for ci in range(3 * GDN_WIDTH // CT_IN):
        c0 = ci * CT_IN
        p = _dot(xn_scr[...], wqkv_ref[:, c0:c0 + CT_IN])
        y = _causal_conv(p, carry_qkv[:, c0:c0 + CT_IN], cw_ref[:, c0:c0 + CT_IN])
        carry_qkv[:, c0:c0 + CT_IN] = p[tm - SUBLANES:tm]
        y = _silu(y)
        which = c0 // GDN_WIDTH
        o0 = c0 - which * GDN_WIDTH
        for hh in range(CT_IN // HEAD_DIM):
            yh = y[:, hh * HEAD_DIM:(hh + 1) * HEAD_DIM]
            if which < 2:
                yh = yh * lax.rsqrt(jnp.sum(yh * yh, axis=-1, keepdims=True) + NORM_EPS)
            if which == 0:
                yh = yh * q_scale
            outs[which][:, o0 + hh * HEAD_DIM:o0 + (hh + 1) * HEAD_DIM] = yh.astype(BF16)

    for ci in range(D_MODEL // CT_IN):
        c0 = ci * CT_IN
        zs_ref[:, c0:c0 + CT_IN] = _silu(_dot(xn_scr[...], wz_ref[:, c0:c0 + CT_IN])).astype(BF16)
        ga_ref[:, c0:c0 + CT_IN] = _sigmoid(_dot(xn_scr[...], wga_ref[:, c0:c0 + CT_IN])).astype(BF16)
        gt_ref[:, c0:c0 + CT_IN] = _sigmoid(_dot(xn_scr[...], wgt_ref[:, c0:c0 + CT_IN])).astype(BF16)

    for ci in range(D_MODEL // CT_IN):
        c0 = ci * CT_IN
        xs = _dot(xn_scr[...], wxs_ref[:, c0:c0 + CT_IN])
        gc = _dot(xn_scr[...], wgc_ref[:, c0:c0 + CT_IN])
        u = gc * xs
        y = _causal_conv(u, carry_u[:, c0:c0 + CT_IN], scw_ref[:, c0:c0 + CT_IN])
        carry_u[:, c0:c0 + CT_IN] = u[tm - SUBLANES:tm]
        gb = _dot(xn_scr[...], wgb_ref[:, c0:c0 + CT_IN])
        sb_ref[:, c0:c0 + CT_IN] = (gb * y).astype(BF16)

    ab = _dot(xn_scr[...], wab_ref[...])
    lane = lax.broadcasted_iota(I32, ab.shape, 1)
    g = -jnp.exp(alog_ref[...]) * _softplus(ab + dtb_ref[...])
    gbeta_ref[...] = jnp.where(lane < HEADS, g, _sigmoid(ab))


def _inproj(x, norm_g, w_in, conv_w, a_log, dt_bias, sconv_w):
    bsz, seq, _ = x.shape
    tm = min(TM_IN, seq)
    o = 0
    w = {}
    for name, width in (("qkv", 3 * GDN_WIDTH), ("z", GDN_WIDTH), ("a", HEADS), ("b", HEADS),
                        ("xs", D_MODEL), ("gb", D_MODEL), ("gc", D_MODEL), ("ga", D_MODEL), ("gt", D_MODEL)):
        w[name] = w_in[:, o:o + width]
        o += width
    wab = jnp.zeros((D_MODEL, LANES), F32).at[:, :HEADS].set(w["a"]).at[:, HEADS:2 * HEADS].set(w["b"])
    alog = jnp.zeros((1, LANES), F32).at[0, :HEADS].set(a_log)
    dtb = jnp.zeros((1, LANES), F32).at[0, :HEADS].set(dt_bias)
    weights = [w["qkv"], w["z"], wab, w["xs"], w["gb"], w["gc"], w["ga"], w["gt"]]
    weights = [a.astype(BF16) for a in weights]

    def const(shape):
        return pl.BlockSpec(shape, lambda b, s: (0,) * len(shape), pipeline_mode=pl.Buffered(1))

    def tok(width):
        return pl.BlockSpec((None, tm, width), lambda b, s: (b, s, 0))

    out_bf = jax.ShapeDtypeStruct((bsz, seq, D_MODEL), BF16)
    return pl.pallas_call(
        _inproj_kernel,
        grid=(bsz, seq // tm),
        in_specs=[tok(D_MODEL), const((1, D_MODEL))] + [const(a.shape) for a in weights]
        + [const((GDN_CONV, 3 * GDN_WIDTH)), const((SC_CONV, D_MODEL)), const((1, LANES)), const((1, LANES))],
        out_specs=[tok(D_MODEL)] * 4 + [tok(LANES)] + [tok(D_MODEL)] * 3,
        out_shape=[out_bf] * 4 + [jax.ShapeDtypeStruct((bsz, seq, LANES), F32)] + [out_bf] * 3,
        scratch_shapes=[pltpu.VMEM((SUBLANES, 3 * GDN_WIDTH), F32), pltpu.VMEM((SUBLANES, D_MODEL), F32),
                        pltpu.VMEM((tm, D_MODEL), BF16)],
        compiler_params=pltpu.CompilerParams(dimension_semantics=("arbitrary", "arbitrary"),
                                             vmem_limit_bytes=VMEM_LIMIT),
        name="inproj",
    )(x, norm_g.reshape(1, D_MODEL), *weights, conv_w, sconv_w, alog, dtb)


def _gdn_kernel(q_ref, k_ref, v_ref, zs_ref, gbeta_ref, ng_ref, o_ref, state_ref):
    ts = q_ref.shape[0]

    @pl.when(pl.program_id(1) == 0)
    def _():
        state_ref[...] = jnp.zeros_like(state_ref)

    ri = lax.broadcasted_iota(I32, (CHUNK, CHUNK), 0)
    ci = lax.broadcasted_iota(I32, (CHUNK, CHUNK), 1)
    causal = ri >= ci
    strict = ri > ci
    ltri = causal.astype(F32)
    eye = (ri == ci).astype(F32)
    ng = ng_ref[...]
    hs = range(HEADS)

    def chunk(c, carry):
        rows = pl.ds(pl.multiple_of(c * CHUNK, CHUNK), CHUNK)
        cols = [slice(h * HEAD_DIM, (h + 1) * HEAD_DIM) for h in hs]
        gbc = gbeta_ref[rows, :]
        gcum = jnp.dot(ltri, gbc, precision=lax.Precision.HIGHEST, preferred_element_type=F32)
        gcum_t = gcum.T
        gc = [gcum[:, h:h + 1] for h in hs]
        gr = [gcum_t[h:h + 1, :] for h in hs]
        beta = [gbc[:, HEADS + h:HEADS + h + 1] for h in hs]
        g_last = [gc[h][CHUNK - 1:CHUNK, :] for h in hs]
        decay = [jnp.where(causal, jnp.exp(jnp.where(causal, gc[h] - gr[h], 0.0)), 0.0) for h in hs]
        eg = [jnp.exp(gc[h]) for h in hs]

        kf = [k_ref[rows, cols[h]].astype(F32) for h in hs]
        kb = [kf[h] * beta[h] for h in hs]
        kt = [kf[h].T for h in hs]
        ktb = [kt[h].astype(BF16) for h in hs]
        lhs = [jnp.concatenate([kb[h].astype(BF16), q_ref[rows, cols[h]]], axis=0) for h in hs]
        aq = [_dot(lhs[h], ktb[h]) for h in hs]
        aqk = [(aq[h][CHUNK:] * decay[h]).astype(BF16) for h in hs]
        m = [jnp.where(strict, -(aq[h][:CHUNK] * decay[h]), 0.0) for h in hs]
        tinv = [eye + m[h] for h in hs]
        mb = [m[h].astype(BF16) for h in hs]
        m = [_dot(mb[h], mb[h]) for h in hs]
        levels = CHUNK.bit_length() - 1
        for j in range(1, levels):
            mb = [m[h].astype(BF16) for h in hs]
            if j < levels - 1:
                st = [_dot(jnp.concatenate([tinv[h].astype(BF16), mb[h]], axis=0), mb[h]) for h in hs]
                tinv = [tinv[h] + st[h][:CHUNK] for h in hs]
                m = [st[h][CHUNK:] for h in hs]
            else:
                tinv = [tinv[h] + _dot(tinv[h].astype(BF16), mb[h]) for h in hs]
        rhs = [jnp.concatenate([(v_ref[rows, cols[h]].astype(F32) * beta[h]).astype(BF16),
                                (kb[h] * eg[h]).astype(BF16)], axis=1) for h in hs]
        uw = [_dot(tinv[h].astype(BF16), rhs[h]) for h in hs]

        sts = [state_ref[h] for h in hs]
        stb = [sts[h].astype(BF16) for h in hs]
        wq = [jnp.concatenate([uw[h][:, HEAD_DIM:].astype(BF16),
                               (q_ref[rows, cols[h]].astype(F32) * eg[h]).astype(BF16)], axis=0) for h in hs]
        ws = [_dot(wq[h], stb[h]) for h in hs]
        vnb = [(uw[h][:, :HEAD_DIM] - ws[h][:CHUNK]).astype(BF16) for h in hs]
        kdt = [(kt[h] * jnp.exp(g_last[h] - gr[h])).astype(BF16) for h in hs]
        o = [ws[h][CHUNK:] + _dot(aqk[h], vnb[h]) for h in hs]
        for h in hs:
            state_ref[h] = sts[h] * jnp.exp(g_last[h]) + _dot(kdt[h], vnb[h])
        for h in hs:
            oh = _rms_norm(o[h], ng) * zs_ref[rows, cols[h]].astype(F32)
            o_ref[rows, cols[h]] = oh.astype(BF16)
        return carry

    lax.fori_loop(0, ts // CHUNK, chunk, 0)


def _gdn(q, k, v, zs, gbeta, norm_g):
    bsz, seq, _ = q.shape
    ts = min(TS_GDN, seq)

    def tok(width):
        return pl.BlockSpec((None, ts, width), lambda b, s: (b, s, 0))

    return pl.pallas_call(
        _gdn_kernel,
        grid=(bsz, seq // ts),
        in_specs=[tok(GDN_WIDTH)] * 4 + [tok(LANES), pl.BlockSpec((1, HEAD_DIM), lambda b, s: (0, 0))],
        out_specs=tok(GDN_WIDTH),
        out_shape=jax.ShapeDtypeStruct((bsz, seq, GDN_WIDTH), BF16),
        scratch_shapes=[pltpu.VMEM((HEADS, HEAD_DIM, HEAD_DIM), F32)],
        compiler_params=pltpu.CompilerParams(dimension_semantics=("arbitrary", "arbitrary"),
                                             vmem_limit_bytes=VMEM_LIMIT),
        name="gdn",
    )(q, k, v, zs, gbeta, norm_g.reshape(1, HEAD_DIM))


def _mix_kernel(og_ref, sb_ref, ga_ref, gt_ref, x_ref, wog_ref, wos_ref, wmix_ref, gffn_ref, wr_ref, br_ref,
                h1_ref, xn2_ref, idx_ref, rank_ref, tw_ref, cnt_ref, carry_ref):
    tm = x_ref.shape[0]

    @pl.when(pl.program_id(0) == 0)
    def _():
        carry_ref[...] = jnp.zeros_like(carry_ref)

    ya = _dot(og_ref[...], wog_ref[...])
    yb = _dot(sb_ref[...], wos_ref[...])
    merged = ga_ref[...].astype(F32) * ya + gt_ref[...].astype(F32) * yb
    h1 = x_ref[...] + _dot(merged.astype(BF16), wmix_ref[...])
    h1_ref[...] = h1
    xn2 = _rms_norm(h1, gffn_ref[...])
    _store_row_tiles(xn2_ref, xn2)

    logits = jnp.dot(xn2, wr_ref[...], precision=lax.Precision.HIGHEST, preferred_element_type=F32) + br_ref[...]
    lane = lax.broadcasted_iota(I32, (tm, LANES), 1)
    lane_f = lane.astype(F32)
    neg_inf = jnp.float32(-jnp.inf)
    work = jnp.where(lane < N_EXPERTS, logits, neg_inf)
    hits, vals = [], []
    for _ in range(TOP_K):
        m = jnp.max(work, axis=-1, keepdims=True)
        first = jnp.min(jnp.where(work == m, lane_f, float(LANES)), axis=-1, keepdims=True)
        hit = lane_f == first
        work = jnp.where(hit, neg_inf, work)
        hits.append(hit)
        vals.append((m, first))
    exps = [jnp.exp(m - vals[0][0]) for m, _ in vals]
    denom = exps[0] + exps[1] + exps[2] + exps[3]

    sel = jnp.zeros((tm, LANES), F32)
    for hit in hits:
        sel = sel + hit.astype(F32)
    ri = lax.broadcasted_iota(I32, (tm, tm), 0)
    ci = lax.broadcasted_iota(I32, (tm, tm), 1)
    before = (ri > ci).astype(BF16)
    rank_mat = _dot(before, sel.astype(BF16)) + carry_ref[...]
    carry_ref[...] = carry_ref[...] + jnp.sum(sel, axis=0, keepdims=True)
    cnt_ref[...] = carry_ref[...]

    idx_out = jnp.zeros((tm, LANES), F32)
    rank_out = jnp.zeros((tm, LANES), F32)
    tw_out = jnp.zeros((tm, LANES), F32)
    for kk in range(TOP_K):
        slot = lane == kk
        rk = jnp.sum(jnp.where(hits[kk], rank_mat, 0.0), axis=-1, keepdims=True)
        idx_out = jnp.where(slot, vals[kk][1], idx_out)
        rank_out = jnp.where(slot, rk, rank_out)
        tw_out = jnp.where(slot, exps[kk] / denom, tw_out)
    idx_ref[...] = idx_out.astype(I32)
    rank_ref[...] = rank_out.astype(I32)
    tw_ref[...] = tw_out


def _mix(og, sb, ga, gt, x, w_o_gdn, w_o_sconv, w_mix_out, norm_ffn_g, w_router, b_router):
    t = x.shape[0]
    tm = min(TM_MIX, t)
    wr = jnp.zeros((D_MODEL, LANES), F32).at[:, :N_EXPERTS].set(w_router)
    br = jnp.zeros((1, LANES), F32).at[0, :N_EXPERTS].set(b_router)

    def const(shape):
        return pl.BlockSpec(shape, lambda i: (0,) * len(shape), pipeline_mode=pl.Buffered(1))

    def tok(width):
        return pl.BlockSpec((tm, width), lambda i: (i, 0))

    return pl.pallas_call(
        _mix_kernel,
        grid=(t // tm,),
        in_specs=[tok(D_MODEL)] * 5 + [const((D_MODEL, D_MODEL))] * 3
        + [const((1, D_MODEL)), const((D_MODEL, LANES)), const((1, LANES))],
        out_specs=[tok(D_MODEL), pl.BlockSpec((tm * ROW_TILES, LANES), lambda i: (i, 0)),
                   tok(LANES), tok(LANES), tok(LANES), pl.BlockSpec((1, LANES), lambda i: (0, 0))],
        out_shape=[jax.ShapeDtypeStruct((t, D_MODEL), F32), jax.ShapeDtypeStruct((t * ROW_TILES, LANES), F32),
                   jax.ShapeDtypeStruct((t, LANES), I32), jax.ShapeDtypeStruct((t, LANES), I32),
                   jax.ShapeDtypeStruct((t, LANES), F32), jax.ShapeDtypeStruct((1, LANES), F32)],
        scratch_shapes=[pltpu.VMEM((1, LANES), F32)],
        compiler_params=pltpu.CompilerParams(dimension_semantics=("arbitrary",), vmem_limit_bytes=VMEM_LIMIT),
        name="mix_router",
    )(og, sb, ga, gt, x, w_o_gdn.astype(BF16), w_o_sconv.astype(BF16), w_mix_out.astype(BF16),
      norm_ffn_g.reshape(1, D_MODEL), wr, br)


def _dispatch_kernel(last_ref, has_ref, dest_ref, x_ref, xpad_ref, zero_ref, sem_z, sem):
    bm = zero_ref.shape[0] // ROW_TILES
    tm = x_ref.shape[0] // ROW_TILES

    @pl.when(pl.program_id(0) == 0)
    def _():
        zero_ref[...] = jnp.zeros_like(zero_ref)
        for e in range(N_EXPERTS):
            @pl.when(has_ref[e] > 0)
            def _():
                start = pl.multiple_of(last_ref[e] * ROW_TILES, bm * ROW_TILES)
                cp = pltpu.make_async_copy(zero_ref, xpad_ref.at[pl.ds(start, bm * ROW_TILES), :], sem_z)
                cp.start()
                cp.wait()

    def issue(g, carry):
        for u in range(ISSUE_UNROLL):
            tok = g * (ISSUE_UNROLL // TOP_K) + u // TOP_K
            pltpu.make_async_copy(_row_tile(x_ref, tok),
                                  _row_tile(xpad_ref, dest_ref[0, 0, g * ISSUE_UNROLL + u]), sem).start()
        return carry

    lax.fori_loop(0, tm * TOP_K // ISSUE_UNROLL, issue, 0)
    for _ in range(TOP_K):
        pltpu.make_async_copy(x_ref, xpad_ref.at[pl.ds(0, tm * ROW_TILES), :], sem).wait()


def _dispatch(xn2_tiles, dest, last_blk, has_blk, n_slots, bm):
    t = xn2_tiles.shape[0] // ROW_TILES
    tm = min(TM_DISP, t)
    nt = t // tm
    grid_spec = pltpu.PrefetchScalarGridSpec(
        num_scalar_prefetch=2,
        grid=(nt,),
        in_specs=[pl.BlockSpec((1, 1, tm * TOP_K), lambda i, a, b: (i, 0, 0), memory_space=pltpu.SMEM),
                  pl.BlockSpec((tm * ROW_TILES, LANES), lambda i, a, b: (i, 0))],
        out_specs=pl.BlockSpec(memory_space=pl.ANY),
        scratch_shapes=[pltpu.VMEM((bm * ROW_TILES, LANES), F32), pltpu.SemaphoreType.DMA(()),
                        pltpu.SemaphoreType.DMA(())],
    )
    return pl.pallas_call(
        _dispatch_kernel,
        grid_spec=grid_spec,
        out_shape=jax.ShapeDtypeStruct((n_slots * ROW_TILES, LANES), F32),
        compiler_params=pltpu.CompilerParams(dimension_semantics=("arbitrary",), vmem_limit_bytes=VMEM_LIMIT,
                                             has_side_effects=True, disable_bounds_checks=True),
        name="dispatch",
    )(last_blk, has_blk, dest.reshape(nt, 1, tm * TOP_K), xn2_tiles)


def _moe_kernel(be_ref, nu_ref, x_ref, wgu_ref, bgu_ref, wd_ref, bd_ref, y_ref):
    bm = x_ref.shape[0] // ROW_TILES

    @pl.when(pl.program_id(0) < nu_ref[0])
    def _():
        xb = jnp.concatenate([_load_row_tiles(x_ref, bm, s).astype(BF16) for s in range(ROW_TILES)], axis=1)
        gu = _dot(xb, wgu_ref[...]) + bgu_ref[...]
        gate = jnp.minimum(gu[:, :D_FF], SWIGLU_LIMIT)
        up = jnp.clip(gu[:, D_FF:], -SWIGLU_LIMIT, SWIGLU_LIMIT)
        hid = (up + 1.0) * gate * _sigmoid(SWIGLU_ALPHA * gate)
        _store_row_tiles(y_ref, _dot(hid.astype(BF16), wd_ref[...]) + bd_ref[...])


def _moe(xpad, block_e, n_used, w_gate_up, b_gate_up, w_down, b_down, bm):
    nb = xpad.shape[0] // (bm * ROW_TILES)

    def rows(i, be, nu):
        return (jnp.minimum(i, nu[0] - 1), 0)

    def expert(i, be, nu):
        return (be[i], 0, 0)

    grid_spec = pltpu.PrefetchScalarGridSpec(
        num_scalar_prefetch=2,
        grid=(nb,),
        in_specs=[pl.BlockSpec((bm * ROW_TILES, LANES), rows),
                  pl.BlockSpec((None, D_MODEL, 2 * D_FF), expert),
                  pl.BlockSpec((None, 1, 2 * D_FF), expert),
                  pl.BlockSpec((None, D_FF, D_MODEL), expert),
                  pl.BlockSpec((None, 1, D_MODEL), expert)],
        out_specs=pl.BlockSpec((bm * ROW_TILES, LANES), rows),
    )
    return pl.pallas_call(
        _moe_kernel,
        grid_spec=grid_spec,
        out_shape=jax.ShapeDtypeStruct(xpad.shape, F32),
        compiler_params=pltpu.CompilerParams(dimension_semantics=("arbitrary",), vmem_limit_bytes=VMEM_LIMIT),
        name="moe_mlp",
    )(block_e, n_used, xpad, w_gate_up.astype(BF16), b_gate_up.reshape(N_EXPERTS, 1, 2 * D_FF),
      w_down.astype(BF16), b_down.reshape(N_EXPERTS, 1, D_MODEL))


def _combine_kernel(dest_ref, dest_next_ref, h1_ref, tw_ref, gf_ref, ypad_ref, out_ref, buf_ref, sem):
    tm = h1_ref.shape[0]
    i = pl.program_id(0)
    cur = i % 2

    def gather(d_ref, parity):
        def issue(g, carry):
            for u in range(ISSUE_UNROLL):
                tok = g * (ISSUE_UNROLL // TOP_K) + u // TOP_K
                pltpu.make_async_copy(_row_tile(ypad_ref, d_ref[0, 0, g * ISSUE_UNROLL + u]),
                                      _row_tile(buf_ref.at[parity * TOP_K + u % TOP_K], tok),
                                      sem.at[parity]).start()
            return carry

        lax.fori_loop(0, tm * TOP_K // ISSUE_UNROLL, issue, 0)

    @pl.when(i == 0)
    def _():
        gather(dest_ref, 0)

    @pl.when(i + 1 < pl.num_programs(0))
    def _():
        gather(dest_next_ref, 1 - cur)

    for kk in range(TOP_K):
        pltpu.make_async_copy(ypad_ref.at[pl.ds(0, tm * ROW_TILES), :], buf_ref.at[cur * TOP_K + kk],
                              sem.at[cur]).wait()

    tw = tw_ref[...]
    accs = []
    ssq = jnp.zeros((tm, 1), F32)
    for s in range(ROW_TILES):
        acc = h1_ref[:, s * LANES:(s + 1) * LANES]
        for kk in range(TOP_K):
            acc = acc + tw[:, kk:kk + 1] * _load_row_tiles(buf_ref.at[cur * TOP_K + kk], tm, s)
        ssq = ssq + jnp.sum(acc * acc, axis=-1, keepdims=True)
        accs.append(acc)
    inv = lax.rsqrt(ssq * (1.0 / D_MODEL) + NORM_EPS)
    for s in range(ROW_TILES):
        out_ref[:, s * LANES:(s + 1) * LANES] = accs[s] * inv * gf_ref[:, s * LANES:(s + 1) * LANES]


def _combine(ypad, dest, h1, tw, norm_final_g):
    t = h1.shape[0]
    tm = min(TM_COMB, t)
    nt = t // tm
    dest3 = dest.reshape(nt, 1, tm * TOP_K)
    return pl.pallas_call(
        _combine_kernel,
        grid=(nt,),
        in_specs=[pl.BlockSpec((1, 1, tm * TOP_K), lambda i: (i, 0, 0), memory_space=pltpu.SMEM),
                  pl.BlockSpec((1, 1, tm * TOP_K), lambda i: (jnp.minimum(i + 1, nt - 1), 0, 0),
                               memory_space=pltpu.SMEM),
                  pl.BlockSpec((tm, D_MODEL), lambda i: (i, 0)),
                  pl.BlockSpec((tm, LANES), lambda i: (i, 0)),
                  pl.BlockSpec((1, D_MODEL), lambda i: (0, 0)),
                  pl.BlockSpec(memory_space=pl.ANY)],
        out_specs=pl.BlockSpec((tm, D_MODEL), lambda i: (i, 0)),
        out_shape=jax.ShapeDtypeStruct((t, D_MODEL), F32),
        scratch_shapes=[pltpu.VMEM((2 * TOP_K, tm * ROW_TILES, LANES), F32), pltpu.SemaphoreType.DMA((2,))],
        compiler_params=pltpu.CompilerParams(dimension_semantics=("arbitrary",), vmem_limit_bytes=VMEM_LIMIT,
                                             disable_bounds_checks=True),
        name="combine",
    )(dest3, dest3, h1, tw, norm_final_g.reshape(1, D_MODEL), ypad)


def _routing_tables(idx_m, rank_m, cnt, t, bm):
    counts = cnt[0, :N_EXPERTS].astype(I32)
    pcounts = ((counts + bm - 1) // bm) * bm
    pend = jnp.cumsum(pcounts)
    pstart = pend - pcounts
    dest = pstart[idx_m[:, :TOP_K]] + rank_m[:, :TOP_K]
    n_slots = ((t * TOP_K + N_EXPERTS * (bm - 1) + bm - 1) // bm) * bm
    nb = n_slots // bm
    n_used = pend[-1] // bm
    starts = jnp.arange(nb, dtype=I32) * bm
    block_e = jnp.minimum(jnp.sum((pend[None, :] <= starts[:, None]).astype(I32), axis=1), N_EXPERTS - 1)
    block_e = jnp.where(jnp.arange(nb) < n_used, block_e, block_e[jnp.maximum(n_used - 1, 0)]).astype(I32)
    last_blk = jnp.maximum(pend - bm, 0).astype(I32)
    has_blk = (pcounts > 0).astype(I32)
    return dest.astype(I32), block_e, n_used.reshape(1).astype(I32), last_blk, has_blk, n_slots


def kernel(x, norm_mix_g, w_in, gdn_conv_w, gdn_A_log, gdn_dt_bias, gdn_norm_g, w_o_gdn, sconv_w, w_o_sconv,
           w_mix_out, norm_ffn_g, w_router, b_router, w_gate_up, b_gate_up, w_down, b_down, norm_final_g):
    bsz, seq, _ = x.shape
    t = bsz * seq
    assert norm_mix_g.shape[0] == 1, "single-layer block only"
    q, k, v, zs, gbeta, sb, ga, gt = _inproj(x, norm_mix_g[0], w_in[0], gdn_conv_w[0], gdn_A_log[0],
                                             gdn_dt_bias[0], sconv_w[0])
    og = _gdn(q, k, v, zs, gbeta, gdn_norm_g[0])

    def flat(a):
        return a.reshape(t, a.shape[-1])

    h1, xn2, idx_m, rank_m, tw, cnt = _mix(flat(og), flat(sb), flat(ga), flat(gt), flat(x), w_o_gdn[0],
                                           w_o_sconv[0], w_mix_out[0], norm_ffn_g[0], w_router[0], b_router[0])
    bm = BM_MOE
    dest, block_e, n_used, last_blk, has_blk, n_slots = _routing_tables(idx_m, rank_m, cnt, t, bm)
    xpad = _dispatch(xn2, dest, last_blk, has_blk, n_slots, bm)
    ypad = _moe(xpad, block_e, n_used, w_gate_up[0], b_gate_up[0], w_down[0], b_down[0], bm)
    return _combine(ypad, dest, h1, tw, norm_final_g).reshape(bsz, seq, D_MODEL)
```

```python
import functools

import jax
import jax.numpy as jnp
from jax import lax
from jax.experimental import pallas as pl
from jax.experimental.pallas import tpu as pltpu

F32 = jnp.float32
BF16 = jnp.bfloat16
I32 = jnp.int32

D_MODEL = 1024
HEADS = 8
HEAD_DIM = 128
GDN_WIDTH = HEADS * HEAD_DIM
GDN_CONV = 4
CHUNK = 128
INV_BLOCK = 32
SC_CONV = 3
N_EXPERTS = 32
TOP_K = 4
D_FF = 1024
SWIGLU_LIMIT = 7.0
SWIGLU_ALPHA = 1.702
NORM_EPS = 1e-6

LANES = 128
SUBLANES = 8
VMEM_LIMIT = 56 * 1024 * 1024

TM_IN = 512
CT_IN = 256
TS_GDN = 512
TM_MIX = 512
TM_DISP = 512
BM_MOE = 512
TM_COMB = 256
ISSUE_UNROLL = 8
CAST_ROWS = 128


def _sigmoid(x):
    return 1.0 / (1.0 + jnp.exp(-x))


def _silu(x):
    return x * _sigmoid(x)


def _softplus(x):
    return jnp.maximum(x, 0.0) + jnp.log(1.0 + jnp.exp(-jnp.abs(x)))


def _rms_norm(x, g):
    return x * lax.rsqrt(jnp.mean(x * x, axis=-1, keepdims=True) + NORM_EPS) * g


def _dot(a, b):
    return jnp.dot(a, b, preferred_element_type=F32)


def _dot_nt(a, b):
    return lax.dot_general(a, b, (((1,), (1,)), ((), ())), preferred_element_type=F32)


def _dot_tn(a, b):
    return lax.dot_general(a, b, (((0,), (0,)), ((), ())), preferred_element_type=F32)


ROW_TILES = D_MODEL // LANES


def _store_row_tiles(ref, val):
    m = val.shape[0]
    for s in range(ROW_TILES):
        ref[pl.ds(s, m, stride=ROW_TILES), :] = val[:, s * LANES:(s + 1) * LANES]


def _load_row_tiles(ref, m, s):
    return ref[pl.ds(s, m, stride=ROW_TILES), :]


def _row_tile(ref, row):
    return ref.at[pl.ds(pl.multiple_of(row * ROW_TILES, ROW_TILES), ROW_TILES), :]


def _causal_conv(p, prev, w):
    taps = w.shape[0]
    row8 = lax.broadcasted_iota(I32, (SUBLANES, p.shape[1]), 0)
    p8 = p[0:SUBLANES]
    y = w[taps - 1:taps] * p
    yh = w[taps - 1:taps] * p8
    for s in range(1, taps):
        wj = w[taps - 1 - s:taps - s]
        y = y + wj * pltpu.roll(p, s, 0)
        hs = jnp.where(row8 < s, pltpu.roll(prev, s, 0), pltpu.roll(p8, s, 0))
        yh = yh + wj * hs
    return jnp.concatenate([yh, y[SUBLANES:]], axis=0)


def _inproj_kernel(x_ref, ng_ref, wqkv_ref, wz_ref, wab_ref, wxs_ref, wgb_ref, wgc_ref, wga_ref, wgt_ref,
                   cw_ref, scw_ref, alog_ref, dtb_ref,
                   q_ref, k_ref, v_ref, zs_ref, gbeta_ref, sb_ref, ga_ref, gt_ref,
                   carry_qkv, carry_u, xn_scr):
    tm = x_ref.shape[0]

    @pl.when(pl.program_id(1) == 0)
    def _():
        carry_qkv[...] = jnp.zeros_like(carry_qkv)
        carry_u[...] = jnp.zeros_like(carry_u)

    x = x_ref[...]
    xn_scr[...] = _rms_norm(x, ng_ref[...]).astype(BF16)

    q_scale = HEAD_DIM ** -0.5
    outs = (q_ref, k_ref, v_ref)
    for ci in range(3 * GDN_WIDTH // CT_IN):
        c0 = ci * CT_IN
        p = _dot(xn_scr[...], wqkv_ref[:, c0:c0 + CT_IN])
        y = _causal_conv(p, carry_qkv[:, c0:c0 + CT_IN], cw_ref[:, c0:c0 + CT_IN])
        carry_qkv[:, c0:c0 + CT_IN] = p[tm - SUBLANES:tm]
        y = _silu(y)
        which = c0 // GDN_WIDTH
        o0 = c0 - which * GDN_WIDTH
        for hh in range(CT_IN // HEAD_DIM):
            yh = y[:, hh * HEAD_DIM:(hh + 1) * HEAD_DIM]
            if which < 2:
                yh = yh * lax.rsqrt(jnp.sum(yh * yh, axis=-1, keepdims=True) + NORM_EPS)
            if which == 0:
                yh = yh * q_scale
            outs[which][:, o0 + hh * HEAD_DIM:o0 + (hh + 1) * HEAD_DIM] = yh.astype(BF16)

    for ci in range(D_MODEL // CT_IN):
        c0 = ci * CT_IN
        zs_ref[:, c0:c0 + CT_IN] = _silu(_dot(xn_scr[...], wz_ref[:, c0:c0 + CT_IN])).astype(BF16)
        ga_ref[:, c0:c0 + CT_IN] = _sigmoid(_dot(xn_scr[...], wga_ref[:, c0:c0 + CT_IN])).astype(BF16)
        gt_ref[:, c0:c0 + CT_IN] = _sigmoid(_dot(xn_scr[...], wgt_ref[:, c0:c0 + CT_IN])).astype(BF16)

    for ci in range(D_MODEL // CT_IN):
        c0 = ci * CT_IN
        xs = _dot(xn_scr[...], wxs_ref[:, c0:c0 + CT_IN])
        gc = _dot(xn_scr[...], wgc_ref[:, c0:c0 + CT_IN])
        u = gc * xs
        y = _causal_conv(u, carry_u[:, c0:c0 + CT_IN], scw_ref[:, c0:c0 + CT_IN])
        carry_u[:, c0:c0 + CT_IN] = u[tm - SUBLANES:tm]
        gb = _dot(xn_scr[...], wgb_ref[:, c0:c0 + CT_IN])
        sb_ref[:, c0:c0 + CT_IN] = (gb * y).astype(BF16)

    ab = _dot(xn_scr[...], wab_ref[...])
    lane = lax.broadcasted_iota(I32, ab.shape, 1)
    g = -jnp.exp(alog_ref[...]) * _softplus(ab + dtb_ref[...])
    gbeta_ref[...] = jnp.where(lane < HEADS, g, _sigmoid(ab))


def _inproj(x, norm_g, w_in, conv_w, a_log, dt_bias, sconv_w):
    bsz, seq, _ = x.shape
    tm = min(TM_IN, seq)
    o = 0
    w = {}
    for name, width in (("qkv", 3 * GDN_WIDTH), ("z", GDN_WIDTH), ("a", HEADS), ("b", HEADS),
                        ("xs", D_MODEL), ("gb", D_MODEL), ("gc", D_MODEL), ("ga", D_MODEL), ("gt", D_MODEL)):
        w[name] = w_in[:, o:o + width]
        o += width
    wab = jnp.zeros((D_MODEL, LANES), F32).at[:, :HEADS].set(w["a"]).at[:, HEADS:2 * HEADS].set(w["b"])
    alog = jnp.zeros((1, LANES), F32).at[0, :HEADS].set(a_log)
    dtb = jnp.zeros((1, LANES), F32).at[0, :HEADS].set(dt_bias)
    weights = [w["qkv"], w["z"], wab, w["xs"], w["gb"], w["gc"], w["ga"], w["gt"]]
    weights = [a.astype(BF16) for a in weights]

    def const(shape):
        return pl.BlockSpec(shape, lambda b, s: (0,) * len(shape), pipeline_mode=pl.Buffered(1))

    def tok(width):
        return pl.BlockSpec((None, tm, width), lambda b, s: (b, s, 0))

    out_bf = jax.ShapeDtypeStruct((bsz, seq, D_MODEL), BF16)
    return pl.pallas_call(
        _inproj_kernel,
        grid=(bsz, seq // tm),
        in_specs=[tok(D_MODEL), const((1, D_MODEL))] + [const(a.shape) for a in weights]
        + [const((GDN_CONV, 3 * GDN_WIDTH)), const((SC_CONV, D_MODEL)), const((1, LANES)), const((1, LANES))],
        out_specs=[tok(D_MODEL)] * 4 + [tok(LANES)] + [tok(D_MODEL)] * 3,
        out_shape=[out_bf] * 4 + [jax.ShapeDtypeStruct((bsz, seq, LANES), F32)] + [out_bf] * 3,
        scratch_shapes=[pltpu.VMEM((SUBLANES, 3 * GDN_WIDTH), F32), pltpu.VMEM((SUBLANES, D_MODEL), F32),
                        pltpu.VMEM((tm, D_MODEL), BF16)],
        compiler_params=pltpu.CompilerParams(dimension_semantics=("arbitrary", "arbitrary"),
                                             vmem_limit_bytes=VMEM_LIMIT),
        name="inproj",
    )(x, norm_g.reshape(1, D_MODEL), *weights, conv_w, sconv_w, alog, dtb)


def _gdn_kernel(q_ref, k_ref, v_ref, zs_ref, gbeta_ref, ng_ref, o_ref, state_ref):
    ts = q_ref.shape[0]

    @pl.when(pl.program_id(1) == 0)
    def _():
        state_ref[...] = jnp.zeros_like(state_ref)

    ri = lax.broadcasted_iota(I32, (CHUNK, CHUNK), 0)
    ci = lax.broadcasted_iota(I32, (CHUNK, CHUNK), 1)
    causal = ri >= ci
    strict = ri > ci
    ltri = causal.astype(F32)
    eye = (ri == ci).astype(F32)
    sh = INV_BLOCK.bit_length() - 1
    diag_blk = (ri >> sh) == (ci >> sh)
    off_blks = []
    while (1 << sh) < CHUNK:
        off_blks.append(((ri >> (sh + 1)) == (ci >> (sh + 1))) & ((ri >> sh) != (ci >> sh)) & strict)
        sh += 1
    ng = ng_ref[...]
    hs = range(HEADS)

    def chunk(c, carry):
        rows = pl.ds(pl.multiple_of(c * CHUNK, CHUNK), CHUNK)
        cols = [slice(h * HEAD_DIM, (h + 1) * HEAD_DIM) for h in hs]
        gbc = gbeta_ref[rows, :]
        gcum = jnp.dot(ltri, gbc, precision=lax.Precision.HIGHEST, preferred_element_type=F32)
        gcum_t = gcum.T
        gc = [gcum[:, h:h + 1] for h in hs]
        gr = [gcum_t[h:h + 1, :] for h in hs]
        beta = [gbc[:, HEADS + h:HEADS + h + 1] for h in hs]
        g_last = [gc[h][CHUNK - 1:CHUNK, :] for h in hs]
        decay = [jnp.where(causal, jnp.exp(jnp.where(causal, gc[h] - gr[h], 0.0)), 0.0) for h in hs]
        eg = [jnp.exp(gc[h]) for h in hs]

        kf = [k_ref[rows, cols[h]].astype(F32) for h in hs]
        kb = [kf[h] * beta[h] for h in hs]
        kt = [kf[h].T for h in hs]
        ktb = [kt[h].astype(BF16) for h in hs]
        lhs = [jnp.concatenate([kb[h].astype(BF16), q_ref[rows, cols[h]]], axis=0) for h in hs]
        aq = [_dot(lhs[h], ktb[h]) for h in hs]
        aqk = [(aq[h][CHUNK:] * decay[h]).astype(BF16) for h in hs]
        lm = [jnp.where(strict, aq[h][:CHUNK] * decay[h], 0.0) for h in hs]
        m = [jnp.where(diag_blk, -lm[h], 0.0) for h in hs]
        tinv = [eye + m[h] for h in hs]
        mb = [m[h].astype(BF16) for h in hs]
        m = [_dot(mb[h], mb[h]) for h in hs]
        levels = INV_BLOCK.bit_length() - 1
        for j in range(1, levels):
            mb = [m[h].astype(BF16) for h in hs]
            if j < levels - 1:
                st = [_dot(jnp.concatenate([tinv[h].astype(BF16), mb[h]], axis=0), mb[h]) for h in hs]
                tinv = [tinv[h] + st[h][:CHUNK] for h in hs]
                m = [st[h][CHUNK:] for h in hs]
            else:
                tinv = [tinv[h] + _dot(tinv[h].astype(BF16), mb[h]) for h in hs]
        for off in off_blks:
            tb = [tinv[h].astype(BF16) for h in hs]
            ct = [_dot(jnp.where(off, lm[h], 0.0).astype(BF16), tb[h]) for h in hs]
            tinv = [tinv[h] - _dot(tb[h], ct[h].astype(BF16)) for h in hs]
        rhs = [jnp.concatenate([(v_ref[rows, cols[h]].astype(F32) * beta[h]).astype(BF16),
                                (kb[h] * eg[h]).astype(BF16)], axis=1) for h in hs]
        uw = [_dot(tinv[h].astype(BF16), rhs[h]) for h in hs]

        sts = [state_ref[h] for h in hs]
        stb = [sts[h].astype(BF16) for h in hs]
        wq = [jnp.concatenate([uw[h][:, HEAD_DIM:].astype(BF16),
                               (q_ref[rows, cols[h]].astype(F32) * eg[h]).astype(BF16)], axis=0) for h in hs]
        ws = [_dot(wq[h], stb[h]) for h in hs]
        vnb = [(uw[h][:, :HEAD_DIM] - ws[h][:CHUNK]).astype(BF16) for h in hs]
        kdt = [(kt[h] * jnp.exp(g_last[h] - gr[h])).astype(BF16) for h in hs]
        o = [ws[h][CHUNK:] + _dot(aqk[h], vnb[h]) for h in hs]
        for h in hs:
            state_ref[h] = sts[h] * jnp.exp(g_last[h]) + _dot(kdt[h], vnb[h])
        for h in hs:
            oh = _rms_norm(o[h], ng) * zs_ref[rows, cols[h]].astype(F32)
            o_ref[rows, cols[h]] = oh.astype(BF16)
        return carry

    lax.fori_loop(0, ts // CHUNK, chunk, 0)


def _gdn(q, k, v, zs, gbeta, norm_g):
    bsz, seq, _ = q.shape
    ts = min(TS_GDN, seq)

    def tok(width):
        return pl.BlockSpec((None, ts, width), lambda b, s: (b, s, 0))

    return pl.pallas_call(
        _gdn_kernel,
        grid=(bsz, seq // ts),
        in_specs=[tok(GDN_WIDTH)] * 4 + [tok(LANES), pl.BlockSpec((1, HEAD_DIM), lambda b, s: (0, 0))],
        out_specs=tok(GDN_WIDTH),
        out_shape=jax.ShapeDtypeStruct((bsz, seq, GDN_WIDTH), BF16),
        scratch_shapes=[pltpu.VMEM((HEADS, HEAD_DIM, HEAD_DIM), F32)],
        compiler_params=pltpu.CompilerParams(dimension_semantics=("arbitrary", "arbitrary"),
                                             vmem_limit_bytes=VMEM_LIMIT),
        name="gdn",
    )(q, k, v, zs, gbeta, norm_g.reshape(1, HEAD_DIM))


def _mix_kernel(og_ref, sb_ref, ga_ref, gt_ref, x_ref, wog_ref, wos_ref, wmix_ref, gffn_ref, wr_ref, br_ref,
                h1_ref, xn2_ref, idx_ref, rank_ref, tw_ref, cnt_ref, carry_ref):
    tm = x_ref.shape[0]

    @pl.when(pl.program_id(0) == 0)
    def _():
        carry_ref[...] = jnp.zeros_like(carry_ref)

    ya = _dot(og_ref[...], wog_ref[...])
    yb = _dot(sb_ref[...], wos_ref[...])
    merged = ga_ref[...].astype(F32) * ya + gt_ref[...].astype(F32) * yb
    h1 = x_ref[...] + _dot(merged.astype(BF16), wmix_ref[...])
    h1_ref[...] = h1
    xn2 = _rms_norm(h1, gffn_ref[...])
    _store_row_tiles(xn2_ref, xn2)

    x_hi = xn2.astype(BF16)
    x_lo = (xn2 - x_hi.astype(F32)).astype(BF16)
    parts = _dot(x_hi, wr_ref[...]) + _dot(x_lo, wr_ref[...])
    logits = parts[:, :LANES] + parts[:, LANES:] + br_ref[...]
    lane = lax.broadcasted_iota(I32, (tm, LANES), 1)
    lane_f = lane.astype(F32)
    neg_inf = jnp.float32(-jnp.inf)
    work = jnp.where(lane < N_EXPERTS, logits, neg_inf)
    hits, vals = [], []
    for _ in range(TOP_K):
        m = jnp.max(work, axis=-1, keepdims=True)
        first = jnp.min(jnp.where(work == m, lane_f, float(LANES)), axis=-1, keepdims=True)
        hit = lane_f == first
        work = jnp.where(hit, neg_inf, work)
        hits.append(hit)
        vals.append((m, first))
    exps = [jnp.exp(m - vals[0][0]) for m, _ in vals]
    denom = exps[0] + exps[1] + exps[2] + exps[3]

    sel = jnp.zeros((tm, LANES), F32)
    for hit in hits:
        sel = sel + hit.astype(F32)
    ri = lax.broadcasted_iota(I32, (tm, tm), 0)
    ci = lax.broadcasted_iota(I32, (tm, tm), 1)
    before = (ri > ci).astype(BF16)
    rank_mat = _dot(before, sel.astype(BF16)) + carry_ref[...]
    carry_ref[...] = carry_ref[...] + jnp.sum(sel, axis=0, keepdims=True)
    cnt_ref[...] = carry_ref[...]

    idx_out = jnp.zeros((tm, LANES), F32)
    rank_out = jnp.zeros((tm, LANES), F32)
    tw_out = jnp.zeros((tm, LANES), F32)
    for kk in range(TOP_K):
        slot = lane == kk
        rk = jnp.sum(jnp.where(hits[kk], rank_mat, 0.0), axis=-1, keepdims=True)
        idx_out = jnp.where(slot, vals[kk][1], idx_out)
        rank_out = jnp.where(slot, rk, rank_out)
        tw_out = jnp.where(slot, exps[kk] / denom, tw_out)
    idx_ref[...] = idx_out.astype(I32)
    rank_ref[...] = rank_out.astype(I32)
    tw_ref[...] = tw_out


def _mix(og, sb, ga, gt, x, w_o_gdn, w_o_sconv, w_mix_out, norm_ffn_g, w_router, b_router):
    t = x.shape[0]
    tm = min(TM_MIX, t)
    wr = jnp.zeros((D_MODEL, LANES), F32).at[:, :N_EXPERTS].set(w_router)
    wr_hi = wr.astype(BF16)
    wr = jnp.concatenate([wr_hi, (wr - wr_hi.astype(F32)).astype(BF16)], axis=1)
    br = jnp.zeros((1, LANES), F32).at[0, :N_EXPERTS].set(b_router)

    def const(shape):
        return pl.BlockSpec(shape, lambda i: (0,) * len(shape), pipeline_mode=pl.Buffered(1))

    def tok(width):
        return pl.BlockSpec((tm, width), lambda i: (i, 0))

    return pl.pallas_call(
        _mix_kernel,
        grid=(t // tm,),
        in_specs=[tok(D_MODEL)] * 5 + [const((D_MODEL, D_MODEL))] * 3
        + [const((1, D_MODEL)), const((D_MODEL, 2 * LANES)), const((1, LANES))],
        out_specs=[tok(D_MODEL), pl.BlockSpec((tm * ROW_TILES, LANES), lambda i: (i, 0)),
                   tok(LANES), tok(LANES), tok(LANES), pl.BlockSpec((1, LANES), lambda i: (0, 0))],
        out_shape=[jax.ShapeDtypeStruct((t, D_MODEL), F32), jax.ShapeDtypeStruct((t * ROW_TILES, LANES), F32),
                   jax.ShapeDtypeStruct((t, LANES), I32), jax.ShapeDtypeStruct((t, LANES), I32),
                   jax.ShapeDtypeStruct((t, LANES), F32), jax.ShapeDtypeStruct((1, LANES), F32)],
        scratch_shapes=[pltpu.VMEM((1, LANES), F32)],
        compiler_params=pltpu.CompilerParams(dimension_semantics=("arbitrary",), vmem_limit_bytes=VMEM_LIMIT),
        name="mix_router",
    )(og, sb, ga, gt, x, w_o_gdn.astype(BF16), w_o_sconv.astype(BF16), w_mix_out.astype(BF16),
      norm_ffn_g.reshape(1, D_MODEL), wr, br)


def _dispatch_kernel(last_ref, has_ref, dest_ref, x_ref, xpad_ref, zero_ref, sem_z, sem):
    bm = zero_ref.shape[0] // ROW_TILES
    tm = x_ref.shape[0] // ROW_TILES

    @pl.when(pl.program_id(0) == 0)
    def _():
        zero_ref[...] = jnp.zeros_like(zero_ref)
        for e in range(N_EXPERTS):
            @pl.when(has_ref[e] > 0)
            def _():
                start = pl.multiple_of(last_ref[e] * ROW_TILES, bm * ROW_TILES)
                cp = pltpu.make_async_copy(zero_ref, xpad_ref.at[pl.ds(start, bm * ROW_TILES), :], sem_z)
                cp.start()
                cp.wait()

    def issue(g, carry):
        for u in range(ISSUE_UNROLL):
            tok = g * (ISSUE_UNROLL // TOP_K) + u // TOP_K
            pltpu.make_async_copy(_row_tile(x_ref, tok),
                                  _row_tile(xpad_ref, dest_ref[0, 0, g * ISSUE_UNROLL + u]), sem).start()
        return carry

    lax.fori_loop(0, tm * TOP_K // ISSUE_UNROLL, issue, 0)
    for _ in range(TOP_K):
        pltpu.make_async_copy(x_ref, xpad_ref.at[pl.ds(0, tm * ROW_TILES), :], sem).wait()


def _dispatch(xn2_tiles, dest, last_blk, has_blk, n_slots, bm):
    t = xn2_tiles.shape[0] // ROW_TILES
    tm = min(TM_DISP, t)
    nt = t // tm
    grid_spec = pltpu.PrefetchScalarGridSpec(
        num_scalar_prefetch=2,
        grid=(nt,),
        in_specs=[pl.BlockSpec((1, 1, tm * TOP_K), lambda i, a, b: (i, 0, 0), memory_space=pltpu.SMEM),
                  pl.BlockSpec((tm * ROW_TILES, LANES), lambda i, a, b: (i, 0))],
        out_specs=pl.BlockSpec(memory_space=pl.ANY),
        scratch_shapes=[pltpu.VMEM((bm * ROW_TILES, LANES), F32), pltpu.SemaphoreType.DMA(()),
                        pltpu.SemaphoreType.DMA(())],
    )
    return pl.pallas_call(
        _dispatch_kernel,
        grid_spec=grid_spec,
        out_shape=jax.ShapeDtypeStruct((n_slots * ROW_TILES, LANES), F32),
        compiler_params=pltpu.CompilerParams(dimension_semantics=("arbitrary",), vmem_limit_bytes=VMEM_LIMIT,
                                             has_side_effects=True, disable_bounds_checks=True),
        name="dispatch",
    )(last_blk, has_blk, dest.reshape(nt, 1, tm * TOP_K), xn2_tiles)


def _moe_kernel(be_ref, nu_ref, x_ref, wgu_ref, bgu_ref, wd_ref, bd_ref, y_ref, wgu_bf, wd_bf):
    bm = x_ref.shape[0] // ROW_TILES
    i = pl.program_id(0)

    @pl.when(i < nu_ref[0])
    def _():
        @pl.when((i == 0) | (be_ref[i] != be_ref[jnp.maximum(i - 1, 0)]))
        def _():
            def cast_rows(r, carry):
                rows = pl.ds(pl.multiple_of(r * CAST_ROWS, CAST_ROWS), CAST_ROWS)
                wgu_bf[rows, :] = wgu_ref[rows, :].astype(BF16)
                wd_bf[rows, :] = wd_ref[rows, :].astype(BF16)
                return carry

            lax.fori_loop(0, D_MODEL // CAST_ROWS, cast_rows, 0)

        xb = jnp.concatenate([_load_row_tiles(x_ref, bm, s).astype(BF16) for s in range(ROW_TILES)], axis=1)
        gu = _dot(xb, wgu_bf[...]) + bgu_ref[...]
        gate = jnp.minimum(gu[:, :D_FF], SWIGLU_LIMIT)
        up = jnp.clip(gu[:, D_FF:], -SWIGLU_LIMIT, SWIGLU_LIMIT)
        hid = (up + 1.0) * gate * _sigmoid(SWIGLU_ALPHA * gate)
        _store_row_tiles(y_ref, _dot(hid.astype(BF16), wd_bf[...]) + bd_ref[...])


def _moe(xpad, block_e, n_used, w_gate_up, b_gate_up, w_down, b_down, bm):
    nb = xpad.shape[0] // (bm * ROW_TILES)

    def rows(i, be, nu):
        return (jnp.minimum(i, nu[0] - 1), 0)

    def expert(i, be, nu):
        return (be[i], 0, 0)

    grid_spec = pltpu.PrefetchScalarGridSpec(
        num_scalar_prefetch=2,
        grid=(nb,),
        in_specs=[pl.BlockSpec((bm * ROW_TILES, LANES), rows),
                  pl.BlockSpec((None, D_MODEL, 2 * D_FF), expert),
                  pl.BlockSpec((None, 1, 2 * D_FF), expert),
                  pl.BlockSpec((None, D_FF, D_MODEL), expert),
                  pl.BlockSpec((None, 1, D_MODEL), expert)],
        out_specs=pl.BlockSpec((bm * ROW_TILES, LANES), rows),
        scratch_shapes=[pltpu.VMEM((D_MODEL, 2 * D_FF), BF16), pltpu.VMEM((D_FF, D_MODEL), BF16)],
    )
    return pl.pallas_call(
        _moe_kernel,
        grid_spec=grid_spec,
        out_shape=jax.ShapeDtypeStruct(xpad.shape, F32),
        compiler_params=pltpu.CompilerParams(dimension_semantics=("arbitrary",), vmem_limit_bytes=VMEM_LIMIT),
        name="moe_mlp",
    )(block_e, n_used, xpad, w_gate_up, b_gate_up.reshape(N_EXPERTS, 1, 2 * D_FF),
      w_down, b_down.reshape(N_EXPERTS, 1, D_MODEL))


def _combine_kernel(dest_ref, dest_next_ref, h1_ref, tw_ref, gf_ref, ypad_ref, out_ref, buf_ref, sem):
    tm = h1_ref.shape[0]
    i = pl.program_id(0)
    cur = i % 2

    def gather(d_ref, parity):
        def issue(g, carry):
            for u in range(ISSUE_UNROLL):
                tok = g * (ISSUE_UNROLL // TOP_K) + u // TOP_K
                pltpu.make_async_copy(_row_tile(ypad_ref, d_ref[0, 0, g * ISSUE_UNROLL + u]),
                                      _row_tile(buf_ref.at[parity * TOP_K + u % TOP_K], tok),
                                      sem.at[parity]).start()
            return carry

        lax.fori_loop(0, tm * TOP_K // ISSUE_UNROLL, issue, 0)

    @pl.when(i == 0)
    def _():
        gather(dest_ref, 0)

    @pl.when(i + 1 < pl.num_programs(0))
    def _():
        gather(dest_next_ref, 1 - cur)

    for kk in range(TOP_K):
        pltpu.make_async_copy(ypad_ref.at[pl.ds(0, tm * ROW_TILES), :], buf_ref.at[cur * TOP_K + kk],
                              sem.at[cur]).wait()

    tw = tw_ref[...]
    accs = []
    ssq = jnp.zeros((tm, 1), F32)
    for s in range(ROW_TILES):
        acc = h1_ref[:, s * LANES:(s + 1) * LANES]
        for kk in range(TOP_K):
            acc = acc + tw[:, kk:kk + 1] * _load_row_tiles(buf_ref.at[cur * TOP_K + kk], tm, s)
        ssq = ssq + jnp.sum(acc * acc, axis=-1, keepdims=True)
        accs.append(acc)
    inv = lax.rsqrt(ssq * (1.0 / D_MODEL) + NORM_EPS)
    for s in range(ROW_TILES):
        out_ref[:, s * LANES:(s + 1) * LANES] = accs[s] * inv * gf_ref[:, s * LANES:(s + 1) * LANES]


def _combine(ypad, dest, h1, tw, norm_final_g):
    t = h1.shape[0]
    tm = min(TM_COMB, t)
    nt = t // tm
    dest3 = dest.reshape(nt, 1, tm * TOP_K)
    return pl.pallas_call(
        _combine_kernel,
        grid=(nt,),
        in_specs=[pl.BlockSpec((1, 1, tm * TOP_K), lambda i: (i, 0, 0), memory_space=pltpu.SMEM),
                  pl.BlockSpec((1, 1, tm * TOP_K), lambda i: (jnp.minimum(i + 1, nt - 1), 0, 0),
                               memory_space=pltpu.SMEM),
                  pl.BlockSpec((tm, D_MODEL), lambda i: (i, 0)),
                  pl.BlockSpec((tm, LANES), lambda i: (i, 0)),
                  pl.BlockSpec((1, D_MODEL), lambda i: (0, 0)),
                  pl.BlockSpec(memory_space=pl.ANY)],
        out_specs=pl.BlockSpec((tm, D_MODEL), lambda i: (i, 0)),
        out_shape=jax.ShapeDtypeStruct((t, D_MODEL), F32),
        scratch_shapes=[pltpu.VMEM((2 * TOP_K, tm * ROW_TILES, LANES), F32), pltpu.SemaphoreType.DMA((2,))],
        compiler_params=pltpu.CompilerParams(dimension_semantics=("arbitrary",), vmem_limit_bytes=VMEM_LIMIT,
                                             disable_bounds_checks=True),
        name="combine",
    )(dest3, dest3, h1, tw, norm_final_g.reshape(1, D_MODEL), ypad)


def _routing_tables(idx_m, rank_m, cnt, t, bm):
    counts = cnt[0, :N_EXPERTS].astype(I32)
    pcounts = ((counts + bm - 1) // bm) * bm
    pend = jnp.cumsum(pcounts)
    pstart = pend - pcounts
    dest = pstart[idx_m[:, :TOP_K]] + rank_m[:, :TOP_K]
    n_slots = ((t * TOP_K + N_EXPERTS * (bm - 1) + bm - 1) // bm) * bm
    nb = n_slots // bm
    n_used = pend[-1] // bm
    starts = jnp.arange(nb, dtype=I32) * bm
    block_e = jnp.minimum(jnp.sum((pend[None, :] <= starts[:, None]).astype(I32), axis=1), N_EXPERTS - 1)
    block_e = jnp.where(jnp.arange(nb) < n_used, block_e, block_e[jnp.maximum(n_used - 1, 0)]).astype(I32)
    last_blk = jnp.maximum(pend - bm, 0).astype(I32)
    has_blk = (pcounts > 0).astype(I32)
    return dest.astype(I32), block_e, n_used.reshape(1).astype(I32), last_blk, has_blk, n_slots


def kernel(x, norm_mix_g, w_in, gdn_conv_w, gdn_A_log, gdn_dt_bias, gdn_norm_g, w_o_gdn, sconv_w, w_o_sconv,
           w_mix_out, norm_ffn_g, w_router, b_router, w_gate_up, b_gate_up, w_down, b_down, norm_final_g):
    bsz, seq, _ = x.shape
    t = bsz * seq
    assert norm_mix_g.shape[0] == 1, "single-layer block only"
    q, k, v, zs, gbeta, sb, ga, gt = _inproj(x, norm_mix_g[0], w_in[0], gdn_conv_w[0], gdn_A_log[0],
                                             gdn_dt_bias[0], sconv_w[0])
    og = _gdn(q, k, v, zs, gbeta, gdn_norm_g[0])

    def flat(a):
        return a.reshape(t, a.shape[-1])

    h1, xn2, idx_m, rank_m, tw, cnt = _mix(flat(og), flat(sb), flat(ga), flat(gt), flat(x), w_o_gdn[0],
                                           w_o_sconv[0], w_mix_out[0], norm_ffn_g[0], w_router[0], b_router[0])
    bm = BM_MOE
    dest, block_e, n_used, last_blk, has_blk, n_slots = _routing_tables(idx_m, rank_m, cnt, t, bm)
    xpad = _dispatch(xn2, dest, last_blk, has_blk, n_slots, bm)
    ypad = _moe(xpad, block_e, n_used, w_gate_up[0], b_gate_up[0], w_down[0], b_down[0], bm)
    return _combine(ypad, dest, h1, tw, norm_final_g).reshape(bsz, seq, D_MODEL)
```

```python
import functools

import jax
import jax.numpy as jnp
from jax import lax
from jax.experimental import pallas as pl
from jax.experimental.pallas import tpu as pltpu

F32 = jnp.float32
BF16 = jnp.bfloat16
I32 = jnp.int32

D_MODEL = 1024
HEADS = 8
HEAD_DIM = 128
GDN_WIDTH = HEADS * HEAD_DIM
GDN_CONV = 4
CHUNK = 128
INV_BLOCK = 32
SC_CONV = 3
N_EXPERTS = 32
TOP_K = 4
D_FF = 1024
SWIGLU_LIMIT = 7.0
SWIGLU_ALPHA = 1.702
NORM_EPS = 1e-6

LANES = 128
SUBLANES = 8
VMEM_LIMIT = 56 * 1024 * 1024

TM_IN = 512
CT_IN = 256
TS_GDN = 512
NB_GDN = 2
TM_MIX = 512
TM_DISP = 512
BM_MOE = 512
TM_COMB = 256
ISSUE_UNROLL = 8
CAST_ROWS = 128


NEG_LOG2E = -1.4426950408889634


def _sigmoid(x):
    return 1.0 / (1.0 + jnp.exp2(x * NEG_LOG2E))


def _silu(x):
    return x * _sigmoid(x)


def _softplus(x):
    return jnp.maximum(x, 0.0) + jnp.log(1.0 + jnp.exp(-jnp.abs(x)))


def _rms_norm(x, g):
    return x * lax.rsqrt(jnp.mean(x * x, axis=-1, keepdims=True) + NORM_EPS) * g


def _dot(a, b):
    return jnp.dot(a, b, preferred_element_type=F32)


def _dot_nt(a, b):
    return lax.dot_general(a, b, (((1,), (1,)), ((), ())), preferred_element_type=F32)


def _dot_tn(a, b):
    return lax.dot_general(a, b, (((0,), (0,)), ((), ())), preferred_element_type=F32)


ROW_TILES = D_MODEL // LANES


def _store_row_tiles(ref, val):
    m = val.shape[0]
    for s in range(ROW_TILES):
        ref[pl.ds(s, m, stride=ROW_TILES), :] = val[:, s * LANES:(s + 1) * LANES]


def _load_row_tiles(ref, m, s):
    return ref[pl.ds(s, m, stride=ROW_TILES), :]


def _row_tile(ref, row):
    return ref.at[pl.ds(pl.multiple_of(row * ROW_TILES, ROW_TILES), ROW_TILES), :]


def _causal_conv(p, prev, w, stage_ref):
    taps = w.shape[0]
    tm = p.shape[0]
    stage_ref[0:SUBLANES, :] = prev
    stage_ref[SUBLANES:SUBLANES + tm, :] = p
    y = w[taps - 1:taps] * p
    for s in range(1, taps):
        y = y + w[taps - 1 - s:taps - s] * stage_ref[SUBLANES - s:SUBLANES - s + tm, :]
    return y


def _inproj_kernel(x_ref, ng_ref, wqkv_ref, wz_ref, wab_ref, wxs_ref, wgb_ref, wgc_ref, wga_ref, wgt_ref,
                   cw_ref, scw_ref, alog_ref, dtb_ref,
                   q_ref, k_ref, v_ref, zs_ref, gbeta_ref, sb_ref, ga_ref, gt_ref,
                   carry_qkv, carry_u, xn_scr, stage_ref):
    tm = x_ref.shape[0]

    @pl.when(pl.program_id(1) == 0)
    def _():
        carry_qkv[...] = jnp.zeros_like(carry_qkv)
        carry_u[...] = jnp.zeros_like(carry_u)

    x = x_ref[...]
    xn_scr[...] = _rms_norm(x, ng_ref[...]).astype(BF16)

    q_scale = HEAD_DIM ** -0.5
    outs = (q_ref, k_ref, v_ref)
    for ci in range(3 * GDN_WIDTH // CT_IN):
        c0 = ci * CT_IN
        p = _dot(xn_scr[...], wqkv_ref[:, c0:c0 + CT_IN])
        y = _causal_conv(p, carry_qkv[:, c0:c0 + CT_IN], cw_ref[:, c0:c0 + CT_IN], stage_ref)
        carry_qkv[:, c0:c0 + CT_IN] = p[tm - SUBLANES:tm]
        y = _silu(y)
        which = c0 // GDN_WIDTH
        o0 = c0 - which * GDN_WIDTH
        for hh in range(CT_IN // HEAD_DIM):
            yh = y[:, hh * HEAD_DIM:(hh + 1) * HEAD_DIM]
            if which < 2:
                yh = yh * lax.rsqrt(jnp.sum(yh * yh, axis=-1, keepdims=True) + NORM_EPS)
            if which == 0:
                yh = yh * q_scale
            outs[which][:, o0 + hh * HEAD_DIM:o0 + (hh + 1) * HEAD_DIM] = yh.astype(BF16)

    for ci in range(D_MODEL // CT_IN):
        c0 = ci * CT_IN
        zs_ref[:, c0:c0 + CT_IN] = _silu(_dot(xn_scr[...], wz_ref[:, c0:c0 + CT_IN])).astype(BF16)
        ga_ref[:, c0:c0 + CT_IN] = _sigmoid(_dot(xn_scr[...], wga_ref[:, c0:c0 + CT_IN])).astype(BF16)
        gt_ref[:, c0:c0 + CT_IN] = _sigmoid(_dot(xn_scr[...], wgt_ref[:, c0:c0 + CT_IN])).astype(BF16)

    for ci in range(D_MODEL // CT_IN):
        c0 = ci * CT_IN
        xs = _dot(xn_scr[...], wxs_ref[:, c0:c0 + CT_IN])
        gc = _dot(xn_scr[...], wgc_ref[:, c0:c0 + CT_IN])
        u = gc * xs
        y = _causal_conv(u, carry_u[:, c0:c0 + CT_IN], scw_ref[:, c0:c0 + CT_IN], stage_ref)
        carry_u[:, c0:c0 + CT_IN] = u[tm - SUBLANES:tm]
        gb = _dot(xn_scr[...], wgb_ref[:, c0:c0 + CT_IN])
        sb_ref[:, c0:c0 + CT_IN] = (gb * y).astype(BF16)

    ab = _dot(xn_scr[...], wab_ref[...])
    lane = lax.broadcasted_iota(I32, ab.shape, 1)
    g = -jnp.exp(alog_ref[...]) * _softplus(ab + dtb_ref[...])
    g_hi = g.astype(BF16).astype(F32)
    gbeta_ref[...] = jnp.where(lane < HEADS, g_hi, jnp.where(lane < 2 * HEADS, _sigmoid(ab), g - g_hi))


def _inproj(x, norm_g, w_in, conv_w, a_log, dt_bias, sconv_w):
    bsz, seq, _ = x.shape
    tm = min(TM_IN, seq)
    o = 0
    w = {}
    for name, width in (("qkv", 3 * GDN_WIDTH), ("z", GDN_WIDTH), ("a", HEADS), ("b", HEADS),
                        ("xs", D_MODEL), ("gb", D_MODEL), ("gc", D_MODEL), ("ga", D_MODEL), ("gt", D_MODEL)):
        w[name] = w_in[:, o:o + width]
        o += width
    wab = (jnp.zeros((D_MODEL, LANES), F32).at[:, :HEADS].set(w["a"]).at[:, HEADS:2 * HEADS].set(w["b"])
           .at[:, 2 * HEADS:3 * HEADS].set(w["a"]))
    alog = jnp.zeros((1, LANES), F32).at[0, :HEADS].set(a_log).at[0, 2 * HEADS:3 * HEADS].set(a_log)
    dtb = jnp.zeros((1, LANES), F32).at[0, :HEADS].set(dt_bias).at[0, 2 * HEADS:3 * HEADS].set(dt_bias)
    weights = [w["qkv"], w["z"], wab, w["xs"], w["gb"], w["gc"], w["ga"], w["gt"]]
    weights = [a.astype(BF16) for a in weights]

    def const(shape):
        return pl.BlockSpec(shape, lambda b, s: (0,) * len(shape), pipeline_mode=pl.Buffered(1))

    def tok(width):
        return pl.BlockSpec((None, tm, width), lambda b, s: (b, s, 0))

    out_bf = jax.ShapeDtypeStruct((bsz, seq, D_MODEL), BF16)
    return pl.pallas_call(
        _inproj_kernel,
        grid=(bsz, seq // tm),
        in_specs=[tok(D_MODEL), const((1, D_MODEL))] + [const(a.shape) for a in weights]
        + [const((GDN_CONV, 3 * GDN_WIDTH)), const((SC_CONV, D_MODEL)), const((1, LANES)), const((1, LANES))],
        out_specs=[tok(D_MODEL)] * 4 + [tok(LANES)] + [tok(D_MODEL)] * 3,
        out_shape=[out_bf] * 4 + [jax.ShapeDtypeStruct((bsz, seq, LANES), F32)] + [out_bf] * 3,
        scratch_shapes=[pltpu.VMEM((SUBLANES, 3 * GDN_WIDTH), F32), pltpu.VMEM((SUBLANES, D_MODEL), F32),
                        pltpu.VMEM((tm, D_MODEL), BF16), pltpu.VMEM((SUBLANES + tm, CT_IN), F32)],
        compiler_params=pltpu.CompilerParams(dimension_semantics=("arbitrary", "arbitrary"),
                                             vmem_limit_bytes=VMEM_LIMIT),
        name="inproj",
    )(x, norm_g.reshape(1, D_MODEL), *weights, conv_w, sconv_w, alog, dtb)


def _gdn_kernel(q_ref, k_ref, v_ref, zs_ref, gbeta_ref, ng_ref, o_ref, state_ref):
    nb, ts = q_ref.shape[0], q_ref.shape[1]

    @pl.when(pl.program_id(1) == 0)
    def _():
        state_ref[...] = jnp.zeros_like(state_ref)

    ri = lax.broadcasted_iota(I32, (CHUNK, CHUNK), 0)
    ci = lax.broadcasted_iota(I32, (CHUNK, CHUNK), 1)
    causal = ri >= ci
    strict = ri > ci
    ltri = causal.astype(BF16)
    eye = (ri == ci).astype(F32)
    sh = INV_BLOCK.bit_length() - 1
    diag_blk = (ri >> sh) == (ci >> sh)
    off_blks = []
    while (1 << sh) < CHUNK:
        off_blks.append(((ri >> (sh + 1)) == (ci >> (sh + 1))) & ((ri >> sh) != (ci >> sh)) & strict)
        sh += 1
    ng = ng_ref[...]
    hs = range(nb * HEADS)
    bi = [u // HEADS for u in hs]
    hi = [u % HEADS for u in hs]

    def chunk(c, carry):
        rows = pl.ds(pl.multiple_of(c * CHUNK, CHUNK), CHUNK)
        cols = [slice(hi[h] * HEAD_DIM, (hi[h] + 1) * HEAD_DIM) for h in hs]
        gbc = [gbeta_ref[b, rows, :] for b in range(nb)]
        gcum = [_dot(ltri, gbc[b].astype(BF16)) for b in range(nb)]
        gcum_t = [gcum[b].T for b in range(nb)]
        gc = [gcum[bi[h]][:, hi[h]:hi[h] + 1] + gcum[bi[h]][:, 2 * HEADS + hi[h]:2 * HEADS + hi[h] + 1]
              for h in hs]
        gr = [gcum_t[bi[h]][hi[h]:hi[h] + 1, :] + gcum_t[bi[h]][2 * HEADS + hi[h]:2 * HEADS + hi[h] + 1, :]
              for h in hs]
        beta = [gbc[bi[h]][:, HEADS + hi[h]:HEADS + hi[h] + 1] for h in hs]
        g_last = [gc[h][CHUNK - 1:CHUNK, :] for h in hs]
        decay = [jnp.where(causal, jnp.exp(jnp.where(causal, gc[h] - gr[h], 0.0)), 0.0) for h in hs]
        eg = [jnp.exp(gc[h]) for h in hs]

        kf = [k_ref[bi[h], rows, cols[h]].astype(F32) for h in hs]
        kb = [kf[h] * beta[h] for h in hs]
        kt = [kf[h].T for h in hs]
        ktb = [kt[h].astype(BF16) for h in hs]
        lhs = [jnp.concatenate([kb[h].astype(BF16), q_ref[bi[h], rows, cols[h]]], axis=0) for h in hs]
        aq = [_dot(lhs[h], ktb[h]) for h in hs]
        aqk = [(aq[h][CHUNK:] * decay[h]).astype(BF16) for h in hs]
        lm = [jnp.where(strict, aq[h][:CHUNK] * decay[h], 0.0) for h in hs]
        m = [jnp.where(diag_blk, -lm[h], 0.0) for h in hs]
        tinv = [eye + m[h] for h in hs]
        mb = [m[h].astype(BF16) for h in hs]
        m = [_dot(mb[h], mb[h]) for h in hs]
        levels = INV_BLOCK.bit_length() - 1
        for j in range(1, levels):
            mb = [m[h].astype(BF16) for h in hs]
            if j < levels - 1:
                st = [_dot(jnp.concatenate([tinv[h].astype(BF16), mb[h]], axis=0), mb[h]) for h in hs]
                tinv = [tinv[h] + st[h][:CHUNK] for h in hs]
                m = [st[h][CHUNK:] for h in hs]
            else:
                tinv = [tinv[h] + _dot(tinv[h].astype(BF16), mb[h]) for h in hs]
        for off in off_blks:
            tb = [tinv[h].astype(BF16) for h in hs]
            ct = [_dot(jnp.where(off, lm[h], 0.0).astype(BF16), tb[h]) for h in hs]
            tinv = [tinv[h] - _dot(tb[h], ct[h].astype(BF16)) for h in hs]
        rhs = [jnp.concatenate([(v_ref[bi[h], rows, cols[h]].astype(F32) * beta[h]).astype(BF16),
                                (kb[h] * eg[h]).astype(BF16)], axis=1) for h in hs]
        uw = [_dot(tinv[h].astype(BF16), rhs[h]) for h in hs]

        sts = [state_ref[h] for h in hs]
        stb = [sts[h].astype(BF16) for h in hs]
        wq = [jnp.concatenate([uw[h][:, HEAD_DIM:].astype(BF16),
                               (q_ref[bi[h], rows, cols[h]].astype(F32) * eg[h]).astype(BF16)], axis=0) for h in hs]
        ws = [_dot(wq[h], stb[h]) for h in hs]
        vnb = [(uw[h][:, :HEAD_DIM] - ws[h][:CHUNK]).astype(BF16) for h in hs]
        kdt = [(kt[h] * jnp.exp(g_last[h] - gr[h])).astype(BF16) for h in hs]
        o = [ws[h][CHUNK:] + _dot(aqk[h], vnb[h]) for h in hs]
        for h in hs:
            state_ref[h] = sts[h] * jnp.exp(g_last[h]) + _dot(kdt[h], vnb[h])
        for h in hs:
            oh = _rms_norm(o[h], ng) * zs_ref[bi[h], rows, cols[h]].astype(F32)
            o_ref[bi[h], rows, cols[h]] = oh.astype(BF16)
        return carry

    lax.fori_loop(0, ts // CHUNK, chunk, 0)


def _gdn(q, k, v, zs, gbeta, norm_g):
    bsz, seq, _ = q.shape
    ts = min(TS_GDN, seq)
    nb = NB_GDN if bsz % NB_GDN == 0 else 1

    def tok(width):
        return pl.BlockSpec((nb, ts, width), lambda b, s: (b, s, 0))

    return pl.pallas_call(
        _gdn_kernel,
        grid=(bsz // nb, seq // ts),
        in_specs=[tok(GDN_WIDTH)] * 4 + [tok(LANES), pl.BlockSpec((1, HEAD_DIM), lambda b, s: (0, 0))],
        out_specs=tok(GDN_WIDTH),
        out_shape=jax.ShapeDtypeStruct((bsz, seq, GDN_WIDTH), BF16),
        scratch_shapes=[pltpu.VMEM((nb * HEADS, HEAD_DIM, HEAD_DIM), F32)],
        compiler_params=pltpu.CompilerParams(dimension_semantics=("arbitrary", "arbitrary"),
                                             vmem_limit_bytes=VMEM_LIMIT),
        name="gdn",
    )(q, k, v, zs, gbeta, norm_g.reshape(1, HEAD_DIM))


def _mix_kernel(og_ref, sb_ref, ga_ref, gt_ref, x_ref, wog_ref, wos_ref, wmix_ref, gffn_ref, wr_ref, br_ref,
                h1_ref, xn2_ref, idx_ref, rank_ref, tw_ref, cnt_ref, carry_ref):
    tm = x_ref.shape[0]

    @pl.when(pl.program_id(0) == 0)
    def _():
        carry_ref[...] = jnp.zeros_like(carry_ref)

    ya = _dot(og_ref[...], wog_ref[...])
    yb = _dot(sb_ref[...], wos_ref[...])
    merged = ga_ref[...].astype(F32) * ya + gt_ref[...].astype(F32) * yb
    h1 = x_ref[...] + _dot(merged.astype(BF16), wmix_ref[...])
    h1_ref[...] = h1
    xn2 = _rms_norm(h1, gffn_ref[...])
    _store_row_tiles(xn2_ref, xn2)

    x_hi = xn2.astype(BF16)
    x_lo = (xn2 - x_hi.astype(F32)).astype(BF16)
    parts = _dot(x_hi, wr_ref[...]) + _dot(x_lo, wr_ref[...])
    logits = parts[:, :LANES] + parts[:, LANES:] + br_ref[...]
    lane = lax.broadcasted_iota(I32, (tm, LANES), 1)
    lane_f = lane.astype(F32)
    neg_inf = jnp.float32(-jnp.inf)
    work = jnp.where(lane < N_EXPERTS, logits, neg_inf)
    hits, vals = [], []
    for _ in range(TOP_K):
        m = jnp.max(work, axis=-1, keepdims=True)
        first = jnp.min(jnp.where(work == m, lane_f, float(LANES)), axis=-1, keepdims=True)
        hit = lane_f == first
        work = jnp.where(hit, neg_inf, work)
        hits.append(hit)
        vals.append((m, first))
    exps = [jnp.exp(m - vals[0][0]) for m, _ in vals]
    denom = exps[0] + exps[1] + exps[2] + exps[3]

    sel = jnp.zeros((tm, LANES), F32)
    for hit in hits:
        sel = sel + hit.astype(F32)
    ri = lax.broadcasted_iota(I32, (tm, tm), 0)
    ci = lax.broadcasted_iota(I32, (tm, tm), 1)
    before = (ri > ci).astype(BF16)
    rank_mat = _dot(before, sel.astype(BF16)) + carry_ref[...]
    carry_ref[...] = carry_ref[...] + jnp.sum(sel, axis=0, keepdims=True)
    cnt_ref[...] = carry_ref[...]

    idx_out = jnp.zeros((tm, LANES), F32)
    rank_out = jnp.zeros((tm, LANES), F32)
    tw_out = jnp.zeros((tm, LANES), F32)
    for kk in range(TOP_K):
        slot = lane == kk
        rk = jnp.sum(jnp.where(hits[kk], rank_mat, 0.0), axis=-1, keepdims=True)
        idx_out = jnp.where(slot, vals[kk][1], idx_out)
        rank_out = jnp.where(slot, rk, rank_out)
        tw_out = jnp.where(slot, exps[kk] / denom, tw_out)
    idx_ref[...] = idx_out.astype(I32)
    rank_ref[...] = rank_out.astype(I32)
    tw_ref[...] = tw_out


def _mix(og, sb, ga, gt, x, w_o_gdn, w_o_sconv, w_mix_out, norm_ffn_g, w_router, b_router):
    t = x.shape[0]
    tm = min(TM_MIX, t)
    wr = jnp.zeros((D_MODEL, LANES), F32).at[:, :N_EXPERTS].set(w_router)
    wr_hi = wr.astype(BF16)
    wr = jnp.concatenate([wr_hi, (wr - wr_hi.astype(F32)).astype(BF16)], axis=1)
    br = jnp.zeros((1, LANES), F32).at[0, :N_EXPERTS].set(b_router)

    def const(shape):
        return pl.BlockSpec(shape, lambda i: (0,) * len(shape), pipeline_mode=pl.Buffered(1))

    def tok(width):
        return pl.BlockSpec((tm, width), lambda i: (i, 0))

    return pl.pallas_call(
        _mix_kernel,
        grid=(t // tm,),
        in_specs=[tok(D_MODEL)] * 5 + [const((D_MODEL, D_MODEL))] * 3
        + [const((1, D_MODEL)), const((D_MODEL, 2 * LANES)), const((1, LANES))],
        out_specs=[tok(D_MODEL), pl.BlockSpec((tm * ROW_TILES, LANES), lambda i: (i, 0)),
                   tok(LANES), tok(LANES), tok(LANES), pl.BlockSpec((1, LANES), lambda i: (0, 0))],
        out_shape=[jax.ShapeDtypeStruct((t, D_MODEL), F32), jax.ShapeDtypeStruct((t * ROW_TILES, LANES), F32),
                   jax.ShapeDtypeStruct((t, LANES), I32), jax.ShapeDtypeStruct((t, LANES), I32),
                   jax.ShapeDtypeStruct((t, LANES), F32), jax.ShapeDtypeStruct((1, LANES), F32)],
        scratch_shapes=[pltpu.VMEM((1, LANES), F32)],
        compiler_params=pltpu.CompilerParams(dimension_semantics=("arbitrary",), vmem_limit_bytes=VMEM_LIMIT),
        name="mix_router",
    )(og, sb, ga, gt, x, w_o_gdn.astype(BF16), w_o_sconv.astype(BF16), w_mix_out.astype(BF16),
      norm_ffn_g.reshape(1, D_MODEL), wr, br)


def _dispatch_kernel(last_ref, has_ref, dest_ref, x_ref, xpad_ref, zero_ref, sem_z, sem):
    bm = zero_ref.shape[0] // ROW_TILES
    tm = x_ref.shape[0] // ROW_TILES

    @pl.when(pl.program_id(0) == 0)
    def _():
        zero_ref[...] = jnp.zeros_like(zero_ref)
        for e in range(N_EXPERTS):
            @pl.when(has_ref[e] > 0)
            def _():
                start = pl.multiple_of(last_ref[e] * ROW_TILES, bm * ROW_TILES)
                cp = pltpu.make_async_copy(zero_ref, xpad_ref.at[pl.ds(start, bm * ROW_TILES), :], sem_z)
                cp.start()
                cp.wait()

    def issue(g, carry):
        for u in range(ISSUE_UNROLL):
            tok = g * (ISSUE_UNROLL // TOP_K) + u // TOP_K
            pltpu.make_async_copy(_row_tile(x_ref, tok),
                                  _row_tile(xpad_ref, dest_ref[0, 0, g * ISSUE_UNROLL + u]), sem).start()
        return carry

    lax.fori_loop(0, tm * TOP_K // ISSUE_UNROLL, issue, 0)
    for _ in range(TOP_K):
        pltpu.make_async_copy(x_ref, xpad_ref.at[pl.ds(0, tm * ROW_TILES), :], sem).wait()


def _dispatch(xn2_tiles, dest, last_blk, has_blk, n_slots, bm):
    t = xn2_tiles.shape[0] // ROW_TILES
    tm = min(TM_DISP, t)
    nt = t // tm
    grid_spec = pltpu.PrefetchScalarGridSpec(
        num_scalar_prefetch=2,
        grid=(nt,),
        in_specs=[pl.BlockSpec((1, 1, tm * TOP_K), lambda i, a, b: (i, 0, 0), memory_space=pltpu.SMEM),
                  pl.BlockSpec((tm * ROW_TILES, LANES), lambda i, a, b: (i, 0))],
        out_specs=pl.BlockSpec(memory_space=pl.ANY),
        scratch_shapes=[pltpu.VMEM((bm * ROW_TILES, LANES), F32), pltpu.SemaphoreType.DMA(()),
                        pltpu.SemaphoreType.DMA(())],
    )
    return pl.pallas_call(
        _dispatch_kernel,
        grid_spec=grid_spec,
        out_shape=jax.ShapeDtypeStruct((n_slots * ROW_TILES, LANES), F32),
        compiler_params=pltpu.CompilerParams(dimension_semantics=("arbitrary",), vmem_limit_bytes=VMEM_LIMIT,
                                             has_side_effects=True, disable_bounds_checks=True),
        name="dispatch",
    )(last_blk, has_blk, dest.reshape(nt, 1, tm * TOP_K), xn2_tiles)


def _moe_kernel(be_ref, nu_ref, x_ref, wgu_ref, bgu_ref, wd_ref, bd_ref, y_ref, wgu_bf, wd_bf):
    bm = x_ref.shape[0] // ROW_TILES
    i = pl.program_id(0)

    @pl.when(i < nu_ref[0])
    def _():
        @pl.when((i == 0) | (be_ref[i] != be_ref[jnp.maximum(i - 1, 0)]))
        def _():
            def cast_rows(r, carry):
                rows = pl.ds(pl.multiple_of(r * CAST_ROWS, CAST_ROWS), CAST_ROWS)
                wgu_bf[rows, :] = wgu_ref[rows, :].astype(BF16)
                wd_bf[rows, :] = wd_ref[rows, :].astype(BF16)
                return carry

            lax.fori_loop(0, D_MODEL // CAST_ROWS, cast_rows, 0)

        xb = jnp.concatenate([_load_row_tiles(x_ref, bm, s).astype(BF16) for s in range(ROW_TILES)], axis=1)
        gu = _dot(xb, wgu_bf[...]) + bgu_ref[...]
        gate = jnp.minimum(gu[:, :D_FF], SWIGLU_LIMIT)
        up = jnp.clip(gu[:, D_FF:], -SWIGLU_LIMIT, SWIGLU_LIMIT)
        hid = (up + 1.0) * gate * _sigmoid(SWIGLU_ALPHA * gate)
        _store_row_tiles(y_ref, _dot(hid.astype(BF16), wd_bf[...]) + bd_ref[...])


def _moe(xpad, block_e, n_used, w_gate_up, b_gate_up, w_down, b_down, bm):
    nb = xpad.shape[0] // (bm * ROW_TILES)

    def rows(i, be, nu):
        return (jnp.minimum(i, nu[0] - 1), 0)

    def expert(i, be, nu):
        return (be[i], 0, 0)

    grid_spec = pltpu.PrefetchScalarGridSpec(
        num_scalar_prefetch=2,
        grid=(nb,),
        in_specs=[pl.BlockSpec((bm * ROW_TILES, LANES), rows),
                  pl.BlockSpec((None, D_MODEL, 2 * D_FF), expert),
                  pl.BlockSpec((None, 1, 2 * D_FF), expert),
                  pl.BlockSpec((None, D_FF, D_MODEL), expert),
                  pl.BlockSpec((None, 1, D_MODEL), expert)],
        out_specs=pl.BlockSpec((bm * ROW_TILES, LANES), rows),
        scratch_shapes=[pltpu.VMEM((D_MODEL, 2 * D_FF), BF16), pltpu.VMEM((D_FF, D_MODEL), BF16)],
    )
    return pl.pallas_call(
        _moe_kernel,
        grid_spec=grid_spec,
        out_shape=jax.ShapeDtypeStruct(xpad.shape, F32),
        compiler_params=pltpu.CompilerParams(dimension_semantics=("arbitrary",), vmem_limit_bytes=VMEM_LIMIT),
        name="moe_mlp",
    )(block_e, n_used, xpad, w_gate_up, b_gate_up.reshape(N_EXPERTS, 1, 2 * D_FF),
      w_down, b_down.reshape(N_EXPERTS, 1, D_MODEL))


def _combine_kernel(dest_ref, dest_next_ref, h1_ref, tw_ref, gf_ref, ypad_ref, out_ref, buf_ref, sem):
    tm = h1_ref.shape[0]
    i = pl.program_id(0)
    cur = i % 2

    def gather(d_ref, parity):
        def issue(g, carry):
            for u in range(ISSUE_UNROLL):
                tok = g * (ISSUE_UNROLL // TOP_K) + u // TOP_K
                pltpu.make_async_copy(_row_tile(ypad_ref, d_ref[0, 0, g * ISSUE_UNROLL + u]),
                                      _row_tile(buf_ref.at[parity * TOP_K + u % TOP_K], tok),
                                      sem.at[parity]).start()
            return carry

        lax.fori_loop(0, tm * TOP_K // ISSUE_UNROLL, issue, 0)

    @pl.when(i == 0)
    def _():
        gather(dest_ref, 0)

    @pl.when(i + 1 < pl.num_programs(0))
    def _():
        gather(dest_next_ref, 1 - cur)

    for kk in range(TOP_K):
        pltpu.make_async_copy(ypad_ref.at[pl.ds(0, tm * ROW_TILES), :], buf_ref.at[cur * TOP_K + kk],
                              sem.at[cur]).wait()

    tw = tw_ref[...]
    accs = []
    ssq = jnp.zeros((tm, 1), F32)
    for s in range(ROW_TILES):
        acc = h1_ref[:, s * LANES:(s + 1) * LANES]
        for kk in range(TOP_K):
            acc = acc + tw[:, kk:kk + 1] * _load_row_tiles(buf_ref.at[cur * TOP_K + kk], tm, s)
        ssq = ssq + jnp.sum(acc * acc, axis=-1, keepdims=True)
        accs.append(acc)
    inv = lax.rsqrt(ssq * (1.0 / D_MODEL) + NORM_EPS)
    for s in range(ROW_TILES):
        out_ref[:, s * LANES:(s + 1) * LANES] = accs[s] * inv * gf_ref[:, s * LANES:(s + 1) * LANES]


def _combine(ypad, dest, h1, tw, norm_final_g):
    t = h1.shape[0]
    tm = min(TM_COMB, t)
    nt = t // tm
    dest3 = dest.reshape(nt, 1, tm * TOP_K)
    return pl.pallas_call(
        _combine_kernel,
        grid=(nt,),
        in_specs=[pl.BlockSpec((1, 1, tm * TOP_K), lambda i: (i, 0, 0), memory_space=pltpu.SMEM),
                  pl.BlockSpec((1, 1, tm * TOP_K), lambda i: (jnp.minimum(i + 1, nt - 1), 0, 0),
                               memory_space=pltpu.SMEM),
                  pl.BlockSpec((tm, D_MODEL), lambda i: (i, 0)),
                  pl.BlockSpec((tm, LANES), lambda i: (i, 0)),
                  pl.BlockSpec((1, D_MODEL), lambda i: (0, 0)),
                  pl.BlockSpec(memory_space=pl.ANY)],
        out_specs=pl.BlockSpec((tm, D_MODEL), lambda i: (i, 0)),
        out_shape=jax.ShapeDtypeStruct((t, D_MODEL), F32),
        scratch_shapes=[pltpu.VMEM((2 * TOP_K, tm * ROW_TILES, LANES), F32), pltpu.SemaphoreType.DMA((2,))],
        compiler_params=pltpu.CompilerParams(dimension_semantics=("arbitrary",), vmem_limit_bytes=VMEM_LIMIT,
                                             disable_bounds_checks=True),
        name="combine",
    )(dest3, dest3, h1, tw, norm_final_g.reshape(1, D_MODEL), ypad)


def _routing_tables(idx_m, rank_m, cnt, t, bm):
    counts = cnt[0, :N_EXPERTS].astype(I32)
    pcounts = ((counts + bm - 1) // bm) * bm
    pend = jnp.cumsum(pcounts)
    pstart = pend - pcounts
    dest = pstart[idx_m[:, :TOP_K]] + rank_m[:, :TOP_K]
    n_slots = ((t * TOP_K + N_EXPERTS * (bm - 1) + bm - 1) // bm) * bm
    nb = n_slots // bm
    n_used = pend[-1] // bm
    starts = jnp.arange(nb, dtype=I32) * bm
    block_e = jnp.minimum(jnp.sum((pend[None, :] <= starts[:, None]).astype(I32), axis=1), N_EXPERTS - 1)
    block_e = jnp.where(jnp.arange(nb) < n_used, block_e, block_e[jnp.maximum(n_used - 1, 0)]).astype(I32)
    last_blk = jnp.maximum(pend - bm, 0).astype(I32)
    has_blk = (pcounts > 0).astype(I32)
    return dest.astype(I32), block_e, n_used.reshape(1).astype(I32), last_blk, has_blk, n_slots


def kernel(x, norm_mix_g, w_in, gdn_conv_w, gdn_A_log, gdn_dt_bias, gdn_norm_g, w_o_gdn, sconv_w, w_o_sconv,
           w_mix_out, norm_ffn_g, w_router, b_router, w_gate_up, b_gate_up, w_down, b_down, norm_final_g):
    bsz, seq, _ = x.shape
    t = bsz * seq
    assert norm_mix_g.shape[0] == 1, "single-layer block only"
    q, k, v, zs, gbeta, sb, ga, gt = _inproj(x, norm_mix_g[0], w_in[0], gdn_conv_w[0], gdn_A_log[0],
                                             gdn_dt_bias[0], sconv_w[0])
    og = _gdn(q, k, v, zs, gbeta, gdn_norm_g[0])

    def flat(a):
        return a.reshape(t, a.shape[-1])

    h1, xn2, idx_m, rank_m, tw, cnt = _mix(flat(og), flat(sb), flat(ga), flat(gt), flat(x), w_o_gdn[0],
                                           w_o_sconv[0], w_mix_out[0], norm_ffn_g[0], w_router[0], b_router[0])
    bm = BM_MOE
    dest, block_e, n_used, last_blk, has_blk, n_slots = _routing_tables(idx_m, rank_m, cnt, t, bm)
    xpad = _dispatch(xn2, dest, last_blk, has_blk, n_slots, bm)
    ypad = _moe(xpad, block_e, n_used, w_gate_up[0], b_gate_up[0], w_down[0], b_down[0], bm)
    return _combine(ypad, dest, h1, tw, norm_final_g).reshape(bsz, seq, D_MODEL)
```

```python
import functools

import jax
import jax.numpy as jnp
from jax import lax
from jax.experimental import pallas as pl
from jax.experimental.pallas import tpu as pltpu

F32 = jnp.float32
BF16 = jnp.bfloat16
I32 = jnp.int32

D_MODEL = 1024
HEADS = 8
HEAD_DIM = 128
GDN_WIDTH = HEADS * HEAD_DIM
GDN_CONV = 4
CHUNK = 128
INV_BLOCK = 32
SC_CONV = 3
N_EXPERTS = 32
TOP_K = 4
D_FF = 1024
SWIGLU_LIMIT = 7.0
SWIGLU_ALPHA = 1.702
NORM_EPS = 1e-6

LANES = 128
SUBLANES = 8
VMEM_LIMIT = 56 * 1024 * 1024

TM_IN = 512
CT_IN = 256
TS_GDN = 512
NB_GDN = 2
TM_MIX = 512
MIX_SPLIT = 4
BM_MOE = 512
TM_COMB = 256
ISSUE_UNROLL = 8
CAST_ROWS = 128


NEG_LOG2E = -1.4426950408889634


def _sigmoid(x):
    return 1.0 / (1.0 + jnp.exp2(x * NEG_LOG2E))


def _silu(x):
    return x * _sigmoid(x)


def _softplus(x):
    return jnp.maximum(x, 0.0) + jnp.log(1.0 + jnp.exp(-jnp.abs(x)))


def _rms_norm(x, g):
    return x * lax.rsqrt(jnp.mean(x * x, axis=-1, keepdims=True) + NORM_EPS) * g


def _dot(a, b):
    return jnp.dot(a, b, preferred_element_type=F32)


def _dot_nt(a, b):
    return lax.dot_general(a, b, (((1,), (1,)), ((), ())), preferred_element_type=F32)


def _dot_tn(a, b):
    return lax.dot_general(a, b, (((0,), (0,)), ((), ())), preferred_element_type=F32)


ROW_TILES = D_MODEL // LANES


def _store_row_tiles(ref, val):
    m = val.shape[0]
    for s in range(ROW_TILES):
        ref[pl.ds(s, m, stride=ROW_TILES), :] = val[:, s * LANES:(s + 1) * LANES]


def _load_row_tiles(ref, m, s):
    return ref[pl.ds(s, m, stride=ROW_TILES), :]


def _row_tile(ref, row):
    return ref.at[pl.ds(pl.multiple_of(row * ROW_TILES, ROW_TILES), ROW_TILES), :]


def _causal_conv(p, prev, w, stage_ref):
    taps = w.shape[0]
    tm = p.shape[0]
    stage_ref[0:SUBLANES, :] = prev
    stage_ref[SUBLANES:SUBLANES + tm, :] = p
    y = w[taps - 1:taps] * p
    for s in range(1, taps):
        y = y + w[taps - 1 - s:taps - s] * stage_ref[SUBLANES - s:SUBLANES - s + tm, :]
    return y


def _inproj_kernel(x_ref, ng_ref, wqkv_ref, wz_ref, wab_ref, wxs_ref, wgb_ref, wgc_ref, wga_ref, wgt_ref,
                   cw_ref, scw_ref, alog_ref, dtb_ref,
                   q_ref, k_ref, v_ref, zs_ref, gbeta_ref, sb_ref, ga_ref, gt_ref,
                   carry_qkv, carry_u, xn_scr, stage_ref):
    tm = x_ref.shape[0]

    @pl.when(pl.program_id(1) == 0)
    def _():
        carry_qkv[...] = jnp.zeros_like(carry_qkv)
        carry_u[...] = jnp.zeros_like(carry_u)

    x = x_ref[...]
    xn_scr[...] = _rms_norm(x, ng_ref[...]).astype(BF16)

    q_scale = HEAD_DIM ** -0.5
    outs = (q_ref, k_ref, v_ref)
    for ci in range(3 * GDN_WIDTH // CT_IN):
        c0 = ci * CT_IN
        p = _dot(xn_scr[...], wqkv_ref[:, c0:c0 + CT_IN])
        y = _causal_conv(p, carry_qkv[:, c0:c0 + CT_IN], cw_ref[:, c0:c0 + CT_IN], stage_ref)
        carry_qkv[:, c0:c0 + CT_IN] = p[tm - SUBLANES:tm]
        y = _silu(y)
        which = c0 // GDN_WIDTH
        o0 = c0 - which * GDN_WIDTH
        for hh in range(CT_IN // HEAD_DIM):
            yh = y[:, hh * HEAD_DIM:(hh + 1) * HEAD_DIM]
            if which < 2:
                yh = yh * lax.rsqrt(jnp.sum(yh * yh, axis=-1, keepdims=True) + NORM_EPS)
            if which == 0:
                yh = yh * q_scale
            outs[which][:, o0 + hh * HEAD_DIM:o0 + (hh + 1) * HEAD_DIM] = yh.astype(BF16)

    for ci in range(D_MODEL // CT_IN):
        c0 = ci * CT_IN
        zs_ref[:, c0:c0 + CT_IN] = _silu(_dot(xn_scr[...], wz_ref[:, c0:c0 + CT_IN])).astype(BF16)
        ga_ref[:, c0:c0 + CT_IN] = _sigmoid(_dot(xn_scr[...], wga_ref[:, c0:c0 + CT_IN])).astype(BF16)
        gt_ref[:, c0:c0 + CT_IN] = _sigmoid(_dot(xn_scr[...], wgt_ref[:, c0:c0 + CT_IN])).astype(BF16)

    for ci in range(D_MODEL // CT_IN):
        c0 = ci * CT_IN
        xs = _dot(xn_scr[...], wxs_ref[:, c0:c0 + CT_IN])
        gc = _dot(xn_scr[...], wgc_ref[:, c0:c0 + CT_IN])
        u = gc * xs
        y = _causal_conv(u, carry_u[:, c0:c0 + CT_IN], scw_ref[:, c0:c0 + CT_IN], stage_ref)
        carry_u[:, c0:c0 + CT_IN] = u[tm - SUBLANES:tm]
        gb = _dot(xn_scr[...], wgb_ref[:, c0:c0 + CT_IN])
        sb_ref[:, c0:c0 + CT_IN] = (gb * y).astype(BF16)

    ab = _dot(xn_scr[...], wab_ref[...])
    lane = lax.broadcasted_iota(I32, ab.shape, 1)
    g = -jnp.exp(alog_ref[...]) * _softplus(ab + dtb_ref[...])
    g_hi = g.astype(BF16).astype(F32)
    gbeta_ref[...] = jnp.where(lane < HEADS, g_hi, jnp.where(lane < 2 * HEADS, _sigmoid(ab), g - g_hi))


def _inproj(x, norm_g, w_in, conv_w, a_log, dt_bias, sconv_w):
    bsz, seq, _ = x.shape
    tm = min(TM_IN, seq)
    o = 0
    w = {}
    for name, width in (("qkv", 3 * GDN_WIDTH), ("z", GDN_WIDTH), ("a", HEADS), ("b", HEADS),
                        ("xs", D_MODEL), ("gb", D_MODEL), ("gc", D_MODEL), ("ga", D_MODEL), ("gt", D_MODEL)):
        w[name] = w_in[:, o:o + width]
        o += width
    wab = (jnp.zeros((D_MODEL, LANES), F32).at[:, :HEADS].set(w["a"]).at[:, HEADS:2 * HEADS].set(w["b"])
           .at[:, 2 * HEADS:3 * HEADS].set(w["a"]))
    alog = jnp.zeros((1, LANES), F32).at[0, :HEADS].set(a_log).at[0, 2 * HEADS:3 * HEADS].set(a_log)
    dtb = jnp.zeros((1, LANES), F32).at[0, :HEADS].set(dt_bias).at[0, 2 * HEADS:3 * HEADS].set(dt_bias)
    weights = [w["qkv"], w["z"], wab, w["xs"], w["gb"], w["gc"], w["ga"], w["gt"]]
    weights = [a.astype(BF16) for a in weights]

    def const(shape):
        return pl.BlockSpec(shape, lambda b, s: (0,) * len(shape), pipeline_mode=pl.Buffered(1))

    def tok(width):
        return pl.BlockSpec((None, tm, width), lambda b, s: (b, s, 0))

    out_bf = jax.ShapeDtypeStruct((bsz, seq, D_MODEL), BF16)
    return pl.pallas_call(
        _inproj_kernel,
        grid=(bsz, seq // tm),
        in_specs=[tok(D_MODEL), const((1, D_MODEL))] + [const(a.shape) for a in weights]
        + [const((GDN_CONV, 3 * GDN_WIDTH)), const((SC_CONV, D_MODEL)), const((1, LANES)), const((1, LANES))],
        out_specs=[tok(D_MODEL)] * 4 + [tok(LANES)] + [tok(D_MODEL)] * 3,
        out_shape=[out_bf] * 4 + [jax.ShapeDtypeStruct((bsz, seq, LANES), F32)] + [out_bf] * 3,
        scratch_shapes=[pltpu.VMEM((SUBLANES, 3 * GDN_WIDTH), F32), pltpu.VMEM((SUBLANES, D_MODEL), F32),
                        pltpu.VMEM((tm, D_MODEL), BF16), pltpu.VMEM((SUBLANES + tm, CT_IN), F32)],
        compiler_params=pltpu.CompilerParams(dimension_semantics=("arbitrary", "arbitrary"),
                                             vmem_limit_bytes=VMEM_LIMIT),
        name="inproj",
    )(x, norm_g.reshape(1, D_MODEL), *weights, conv_w, sconv_w, alog, dtb)


def _gdn_kernel(q_ref, k_ref, v_ref, zs_ref, gbeta_ref, ng_ref, o_ref, state_ref):
    nb, ts = q_ref.shape[0], q_ref.shape[1]

    @pl.when(pl.program_id(1) == 0)
    def _():
        state_ref[...] = jnp.zeros_like(state_ref)

    ri = lax.broadcasted_iota(I32, (CHUNK, CHUNK), 0)
    ci = lax.broadcasted_iota(I32, (CHUNK, CHUNK), 1)
    causal = ri >= ci
    strict = ri > ci
    ltri = causal.astype(BF16)
    eye = (ri == ci).astype(F32)
    sh = INV_BLOCK.bit_length() - 1
    diag_blk = (ri >> sh) == (ci >> sh)
    off_blks = []
    while (1 << sh) < CHUNK:
        off_blks.append(((ri >> (sh + 1)) == (ci >> (sh + 1))) & ((ri >> sh) != (ci >> sh)) & strict)
        sh += 1
    ng = ng_ref[...]
    hs = range(nb * HEADS)
    bi = [u // HEADS for u in hs]
    hi = [u % HEADS for u in hs]

    def chunk(c, carry):
        rows = pl.ds(pl.multiple_of(c * CHUNK, CHUNK), CHUNK)
        cols = [slice(hi[h] * HEAD_DIM, (hi[h] + 1) * HEAD_DIM) for h in hs]
        gbc = [gbeta_ref[b, rows, :] for b in range(nb)]
        gcum = [_dot(ltri, gbc[b].astype(BF16)) for b in range(nb)]
        gcum_t = [gcum[b].T for b in range(nb)]
        gc = [gcum[bi[h]][:, hi[h]:hi[h] + 1] + gcum[bi[h]][:, 2 * HEADS + hi[h]:2 * HEADS + hi[h] + 1]
              for h in hs]
        gr = [gcum_t[bi[h]][hi[h]:hi[h] + 1, :] + gcum_t[bi[h]][2 * HEADS + hi[h]:2 * HEADS + hi[h] + 1, :]
              for h in hs]
        beta = [gbc[bi[h]][:, HEADS + hi[h]:HEADS + hi[h] + 1] for h in hs]
        g_last = [gc[h][CHUNK - 1:CHUNK, :] for h in hs]
        decay = [jnp.where(causal, jnp.exp(jnp.where(causal, gc[h] - gr[h], 0.0)), 0.0) for h in hs]
        eg = [jnp.exp(gc[h]) for h in hs]

        kf = [k_ref[bi[h], rows, cols[h]].astype(F32) for h in hs]
        kb = [kf[h] * beta[h] for h in hs]
        kt = [kf[h].T for h in hs]
        ktb = [kt[h].astype(BF16) for h in hs]
        lhs = [jnp.concatenate([kb[h].astype(BF16), q_ref[bi[h], rows, cols[h]]], axis=0) for h in hs]
        aq = [_dot(lhs[h], ktb[h]) for h in hs]
        aqk = [(aq[h][CHUNK:] * decay[h]).astype(BF16) for h in hs]
        lm = [jnp.where(strict, aq[h][:CHUNK] * decay[h], 0.0) for h in hs]
        m = [jnp.where(diag_blk, -lm[h], 0.0) for h in hs]
        tinv = [eye + m[h] for h in hs]
        mb = [m[h].astype(BF16) for h in hs]
        m = [_dot(mb[h], mb[h]) for h in hs]
        levels = INV_BLOCK.bit_length() - 1
        for j in range(1, levels):
            mb = [m[h].astype(BF16) for h in hs]
            if j < levels - 1:
                st = [_dot(jnp.concatenate([tinv[h].astype(BF16), mb[h]], axis=0), mb[h]) for h in hs]
                tinv = [tinv[h] + st[h][:CHUNK] for h in hs]
                m = [st[h][CHUNK:] for h in hs]
            else:
                tinv = [tinv[h] + _dot(tinv[h].astype(BF16), mb[h]) for h in hs]
        for off in off_blks:
            tb = [tinv[h].astype(BF16) for h in hs]
            ct = [_dot(jnp.where(off, lm[h], 0.0).astype(BF16), tb[h]) for h in hs]
            tinv = [tinv[h] - _dot(tb[h], ct[h].astype(BF16)) for h in hs]
        rhs = [jnp.concatenate([(v_ref[bi[h], rows, cols[h]].astype(F32) * beta[h]).astype(BF16),
                                (kb[h] * eg[h]).astype(BF16)], axis=1) for h in hs]
        uw = [_dot(tinv[h].astype(BF16), rhs[h]) for h in hs]

        sts = [state_ref[h] for h in hs]
        stb = [sts[h].astype(BF16) for h in hs]
        wq = [jnp.concatenate([uw[h][:, HEAD_DIM:].astype(BF16),
                               (q_ref[bi[h], rows, cols[h]].astype(F32) * eg[h]).astype(BF16)], axis=0) for h in hs]
        ws = [_dot(wq[h], stb[h]) for h in hs]
        vnb = [(uw[h][:, :HEAD_DIM] - ws[h][:CHUNK]).astype(BF16) for h in hs]
        kdt = [(kt[h] * jnp.exp(g_last[h] - gr[h])).astype(BF16) for h in hs]
        o = [ws[h][CHUNK:] + _dot(aqk[h], vnb[h]) for h in hs]
        for h in hs:
            state_ref[h] = sts[h] * jnp.exp(g_last[h]) + _dot(kdt[h], vnb[h])
        for h in hs:
            oh = _rms_norm(o[h], ng) * zs_ref[bi[h], rows, cols[h]].astype(F32)
            o_ref[bi[h], rows, cols[h]] = oh.astype(BF16)
        return carry

    lax.fori_loop(0, ts // CHUNK, chunk, 0)


def _gdn(q, k, v, zs, gbeta, norm_g):
    bsz, seq, _ = q.shape
    ts = min(TS_GDN, seq)
    nb = NB_GDN if bsz % NB_GDN == 0 else 1

    def tok(width):
        return pl.BlockSpec((nb, ts, width), lambda b, s: (b, s, 0))

    return pl.pallas_call(
        _gdn_kernel,
        grid=(bsz // nb, seq // ts),
        in_specs=[tok(GDN_WIDTH)] * 4 + [tok(LANES), pl.BlockSpec((1, HEAD_DIM), lambda b, s: (0, 0))],
        out_specs=tok(GDN_WIDTH),
        out_shape=jax.ShapeDtypeStruct((bsz, seq, GDN_WIDTH), BF16),
        scratch_shapes=[pltpu.VMEM((nb * HEADS, HEAD_DIM, HEAD_DIM), F32)],
        compiler_params=pltpu.CompilerParams(dimension_semantics=("arbitrary", "arbitrary"),
                                             vmem_limit_bytes=VMEM_LIMIT),
        name="gdn",
    )(q, k, v, zs, gbeta, norm_g.reshape(1, HEAD_DIM))


def _mix_kernel(og_ref, sb_ref, ga_ref, gt_ref, x_ref, wog_ref, wos_ref, wmix_ref, gffn_ref, wr_ref, br_ref,
                h1_ref, xn2_ref, meta_ref, tw_ref, cnt_ref, carry_ref):
    tm = x_ref.shape[0]

    @pl.when(pl.program_id(0) == 0)
    def _():
        carry_ref[...] = jnp.zeros_like(carry_ref)

    sub = tm // MIX_SPLIT
    rs = [slice(r * sub, (r + 1) * sub) for r in range(MIX_SPLIT)]
    ya = [_dot(og_ref[r, :], wog_ref[...]) for r in rs]
    yb = [_dot(sb_ref[r, :], wos_ref[...]) for r in rs]
    merged = [(ga_ref[r, :].astype(F32) * a + gt_ref[r, :].astype(F32) * b).astype(BF16)
              for r, a, b in zip(rs, ya, yb)]
    h1 = [x_ref[r, :] + _dot(mg, wmix_ref[...]) for r, mg in zip(rs, merged)]
    xn2 = [_rms_norm(h, gffn_ref[...]) for h in h1]
    x_hi = [v.astype(BF16) for v in xn2]
    x_lo = [(v - hi.astype(F32)).astype(BF16) for v, hi in zip(xn2, x_hi)]
    parts = [_dot(hi, wr_ref[...]) + _dot(lo, wr_ref[...]) for hi, lo in zip(x_hi, x_lo)]
    for r, h in zip(rs, h1):
        h1_ref[r, :] = h
    _store_row_tiles(xn2_ref, jnp.concatenate(xn2, axis=0))
    logits = jnp.concatenate([p[:, :LANES] + p[:, LANES:] for p in parts], axis=0) + br_ref[...]

    lane = lax.broadcasted_iota(I32, (tm, LANES), 1)
    lane_f = lane.astype(F32)
    neg_inf = jnp.float32(-jnp.inf)
    work = jnp.where(lane < N_EXPERTS, logits, neg_inf)
    hits, vals = [], []
    for _ in range(TOP_K):
        m = jnp.max(work, axis=-1, keepdims=True)
        first = jnp.min(jnp.where(work == m, lane_f, float(LANES)), axis=-1, keepdims=True)
        hit = lane_f == first
        work = jnp.where(hit, neg_inf, work)
        hits.append(hit)
        vals.append((m, first))
    exps = [jnp.exp(m - vals[0][0]) for m, _ in vals]
    denom = exps[0] + exps[1] + exps[2] + exps[3]

    sel = jnp.zeros((tm, LANES), F32)
    for hit in hits:
        sel = sel + hit.astype(F32)
    ri = lax.broadcasted_iota(I32, (tm, tm), 0)
    ci = lax.broadcasted_iota(I32, (tm, tm), 1)
    before = (ri > ci).astype(BF16)
    rank_mat = _dot(before, sel.astype(BF16)) + carry_ref[...]
    carry_ref[...] = carry_ref[...] + jnp.sum(sel, axis=0, keepdims=True)
    cnt_ref[...] = carry_ref[...]

    meta = jnp.zeros((tm, LANES), F32)
    tw_out = jnp.zeros((tm, LANES), F32)
    for kk in range(TOP_K):
        rk = jnp.sum(jnp.where(hits[kk], rank_mat, 0.0), axis=-1, keepdims=True)
        meta = jnp.where(lane == kk, vals[kk][1], meta)
        meta = jnp.where(lane == TOP_K + kk, rk, meta)
        tw_out = jnp.where(lane == kk, exps[kk] / denom, tw_out)
    meta_ref[0] = meta.T[0:2 * TOP_K, :].astype(I32)
    tw_ref[...] = tw_out


def _mix(og, sb, ga, gt, x, w_o_gdn, w_o_sconv, w_mix_out, norm_ffn_g, w_router, b_router):
    t = x.shape[0]
    tm = min(TM_MIX, t)
    nt = t // tm
    wr = jnp.zeros((D_MODEL, LANES), F32).at[:, :N_EXPERTS].set(w_router)
    wr_hi = wr.astype(BF16)
    wr = jnp.concatenate([wr_hi, (wr - wr_hi.astype(F32)).astype(BF16)], axis=1)
    br = jnp.zeros((1, LANES), F32).at[0, :N_EXPERTS].set(b_router)

    def const(shape):
        return pl.BlockSpec(shape, lambda i: (0,) * len(shape), pipeline_mode=pl.Buffered(1))

    def tok(rows, width):
        return pl.BlockSpec((rows, width), lambda i: (i, 0))

    return pl.pallas_call(
        _mix_kernel,
        grid=(nt,),
        in_specs=[tok(tm, D_MODEL)] * 5 + [const((D_MODEL, D_MODEL))] * 3
        + [const((1, D_MODEL)), const((D_MODEL, 2 * LANES)), const((1, LANES))],
        out_specs=[tok(tm, D_MODEL), tok(tm * ROW_TILES, LANES),
                   pl.BlockSpec((1, 2 * TOP_K, tm), lambda i: (i, 0, 0)), tok(tm, LANES),
                   pl.BlockSpec((1, LANES), lambda i: (0, 0))],
        out_shape=[jax.ShapeDtypeStruct((t, D_MODEL), F32), jax.ShapeDtypeStruct((t * ROW_TILES, LANES), F32),
                   jax.ShapeDtypeStruct((nt, 2 * TOP_K, tm), I32), jax.ShapeDtypeStruct((t, LANES), F32),
                   jax.ShapeDtypeStruct((1, LANES), F32)],
        scratch_shapes=[pltpu.VMEM((1, LANES), F32)],
        compiler_params=pltpu.CompilerParams(dimension_semantics=("arbitrary",), vmem_limit_bytes=VMEM_LIMIT),
        name="mix_router",
    )(og, sb, ga, gt, x, w_o_gdn.astype(BF16), w_o_sconv.astype(BF16), w_mix_out.astype(BF16),
      norm_ffn_g.reshape(1, D_MODEL), wr, br)


def _dispatch_kernel(last_ref, has_ref, dest_ref, x_ref, xpad_ref, zero_ref, sem_z, sem):
    bm = zero_ref.shape[0] // ROW_TILES
    tm = x_ref.shape[0] // ROW_TILES

    @pl.when(pl.program_id(0) == 0)
    def _():
        zero_ref[...] = jnp.zeros_like(zero_ref)
        for e in range(N_EXPERTS):
            @pl.when(has_ref[e] > 0)
            def _():
                start = pl.multiple_of(last_ref[e] * ROW_TILES, bm * ROW_TILES)
                cp = pltpu.make_async_copy(zero_ref, xpad_ref.at[pl.ds(start, bm * ROW_TILES), :], sem_z)
                cp.start()
                cp.wait()

    def issue(g, carry):
        for u in range(ISSUE_UNROLL):
            tok = g * (ISSUE_UNROLL // TOP_K) + u // TOP_K
            pltpu.make_async_copy(_row_tile(x_ref, tok),
                                  _row_tile(xpad_ref, dest_ref[0, u % TOP_K, tok]), sem).start()
        return carry

    lax.fori_loop(0, tm * TOP_K // ISSUE_UNROLL, issue, 0)
    for _ in range(TOP_K):
        pltpu.make_async_copy(x_ref, xpad_ref.at[pl.ds(0, tm * ROW_TILES), :], sem).wait()


def _dispatch(xn2_tiles, dest, last_blk, has_blk, n_slots, bm):
    nt, _, tm = dest.shape
    grid_spec = pltpu.PrefetchScalarGridSpec(
        num_scalar_prefetch=2,
        grid=(nt,),
        in_specs=[pl.BlockSpec((1, TOP_K, tm), lambda i, a, b: (i, 0, 0), memory_space=pltpu.SMEM),
                  pl.BlockSpec((tm * ROW_TILES, LANES), lambda i, a, b: (i, 0))],
        out_specs=pl.BlockSpec(memory_space=pl.ANY),
        scratch_shapes=[pltpu.VMEM((bm * ROW_TILES, LANES), F32), pltpu.SemaphoreType.DMA(()),
                        pltpu.SemaphoreType.DMA(())],
    )
    return pl.pallas_call(
        _dispatch_kernel,
        grid_spec=grid_spec,
        out_shape=jax.ShapeDtypeStruct((n_slots * ROW_TILES, LANES), F32),
        compiler_params=pltpu.CompilerParams(dimension_semantics=("arbitrary",), vmem_limit_bytes=VMEM_LIMIT,
                                             has_side_effects=True, disable_bounds_checks=True),
        name="dispatch",
    )(last_blk, has_blk, dest, xn2_tiles)


def _moe_kernel(be_ref, nu_ref, x_ref, wgu_ref, bgu_ref, wd_ref, bd_ref, y_ref, wgu_bf, wd_bf):
    bm = x_ref.shape[0] // ROW_TILES
    i = pl.program_id(0)

    @pl.when(i < nu_ref[0])
    def _():
        @pl.when((i == 0) | (be_ref[i] != be_ref[jnp.maximum(i - 1, 0)]))
        def _():
            def cast_rows(r, carry):
                rows = pl.ds(pl.multiple_of(r * CAST_ROWS, CAST_ROWS), CAST_ROWS)
                wgu_bf[rows, :] = wgu_ref[rows, :].astype(BF16)
                wd_bf[rows, :] = wd_ref[rows, :].astype(BF16)
                return carry

            lax.fori_loop(0, D_MODEL // CAST_ROWS, cast_rows, 0)

        xb = jnp.concatenate([_load_row_tiles(x_ref, bm, s).astype(BF16) for s in range(ROW_TILES)], axis=1)
        gu = _dot(xb, wgu_bf[...]) + bgu_ref[...]
        gate = jnp.minimum(gu[:, :D_FF], SWIGLU_LIMIT)
        up = jnp.clip(gu[:, D_FF:], -SWIGLU_LIMIT, SWIGLU_LIMIT)
        hid = (up + 1.0) * gate * _sigmoid(SWIGLU_ALPHA * gate)
        _store_row_tiles(y_ref, _dot(hid.astype(BF16), wd_bf[...]) + bd_ref[...])


def _moe(xpad, block_e, n_used, w_gate_up, b_gate_up, w_down, b_down, bm):
    nb = xpad.shape[0] // (bm * ROW_TILES)

    def rows(i, be, nu):
        return (jnp.minimum(i, nu[0] - 1), 0)

    def expert(i, be, nu):
        return (be[i], 0, 0)

    grid_spec = pltpu.PrefetchScalarGridSpec(
        num_scalar_prefetch=2,
        grid=(nb,),
        in_specs=[pl.BlockSpec((bm * ROW_TILES, LANES), rows),
                  pl.BlockSpec((None, D_MODEL, 2 * D_FF), expert),
                  pl.BlockSpec((None, 1, 2 * D_FF), expert),
                  pl.BlockSpec((None, D_FF, D_MODEL), expert),
                  pl.BlockSpec((None, 1, D_MODEL), expert)],
        out_specs=pl.BlockSpec((bm * ROW_TILES, LANES), rows),
        scratch_shapes=[pltpu.VMEM((D_MODEL, 2 * D_FF), BF16), pltpu.VMEM((D_FF, D_MODEL), BF16)],
    )
    return pl.pallas_call(
        _moe_kernel,
        grid_spec=grid_spec,
        out_shape=jax.ShapeDtypeStruct(xpad.shape, F32),
        compiler_params=pltpu.CompilerParams(dimension_semantics=("arbitrary",), vmem_limit_bytes=VMEM_LIMIT),
        name="moe_mlp",
    )(block_e, n_used, xpad, w_gate_up, b_gate_up.reshape(N_EXPERTS, 1, 2 * D_FF),
      w_down, b_down.reshape(N_EXPERTS, 1, D_MODEL))


def _combine_kernel(dest_ref, dest_next_ref, h1_ref, tw_ref, gf_ref, ypad_ref, out_ref, buf_ref, sem):
    tm = h1_ref.shape[0]
    i = pl.program_id(0)
    cur = i % 2

    def gather(d_ref, parity):
        def issue(g, carry):
            for u in range(ISSUE_UNROLL):
                tok = g * (ISSUE_UNROLL // TOP_K) + u // TOP_K
                pltpu.make_async_copy(_row_tile(ypad_ref, d_ref[0, u % TOP_K, tok]),
                                      _row_tile(buf_ref.at[parity * TOP_K + u % TOP_K], tok),
                                      sem.at[parity]).start()
            return carry

        lax.fori_loop(0, tm * TOP_K // ISSUE_UNROLL, issue, 0)

    @pl.when(i == 0)
    def _():
        gather(dest_ref, 0)

    @pl.when(i + 1 < pl.num_programs(0))
    def _():
        gather(dest_next_ref, 1 - cur)

    for kk in range(TOP_K):
        pltpu.make_async_copy(ypad_ref.at[pl.ds(0, tm * ROW_TILES), :], buf_ref.at[cur * TOP_K + kk],
                              sem.at[cur]).wait()

    tw = tw_ref[...]
    accs = []
    ssq = jnp.zeros((tm, 1), F32)
    for s in range(ROW_TILES):
        acc = h1_ref[:, s * LANES:(s + 1) * LANES]
        for kk in range(TOP_K):
            acc = acc + tw[:, kk:kk + 1] * _load_row_tiles(buf_ref.at[cur * TOP_K + kk], tm, s)
        ssq = ssq + jnp.sum(acc * acc, axis=-1, keepdims=True)
        accs.append(acc)
    inv = lax.rsqrt(ssq * (1.0 / D_MODEL) + NORM_EPS)
    for s in range(ROW_TILES):
        out_ref[:, s * LANES:(s + 1) * LANES] = accs[s] * inv * gf_ref[:, s * LANES:(s + 1) * LANES]


def _combine(ypad, dest, h1, tw, norm_final_g):
    t = h1.shape[0]
    tm = min(TM_COMB, t)
    nt = t // tm
    per = dest.shape[2] // tm

    def dest_block(i):
        return (i // per, 0, i % per)

    return pl.pallas_call(
        _combine_kernel,
        grid=(nt,),
        in_specs=[pl.BlockSpec((1, TOP_K, tm), dest_block, memory_space=pltpu.SMEM),
                  pl.BlockSpec((1, TOP_K, tm), lambda i: dest_block(jnp.minimum(i + 1, nt - 1)),
                               memory_space=pltpu.SMEM),
                  pl.BlockSpec((tm, D_MODEL), lambda i: (i, 0)),
                  pl.BlockSpec((tm, LANES), lambda i: (i, 0)),
                  pl.BlockSpec((1, D_MODEL), lambda i: (0, 0)),
                  pl.BlockSpec(memory_space=pl.ANY)],
        out_specs=pl.BlockSpec((tm, D_MODEL), lambda i: (i, 0)),
        out_shape=jax.ShapeDtypeStruct((t, D_MODEL), F32),
        scratch_shapes=[pltpu.VMEM((2 * TOP_K, tm * ROW_TILES, LANES), F32), pltpu.SemaphoreType.DMA((2,))],
        compiler_params=pltpu.CompilerParams(dimension_semantics=("arbitrary",), vmem_limit_bytes=VMEM_LIMIT,
                                             disable_bounds_checks=True),
        name="combine",
    )(dest, dest, h1, tw, norm_final_g.reshape(1, D_MODEL), ypad)


def _routing_tables(meta, cnt, t, bm):
    counts = cnt[0, :N_EXPERTS].astype(I32)
    pcounts = ((counts + bm - 1) // bm) * bm
    pend = jnp.cumsum(pcounts)
    pstart = (pend - pcounts).astype(I32)
    idx, rank = meta[:, :TOP_K, :], meta[:, TOP_K:, :]
    onehot = idx[..., None] == jnp.arange(N_EXPERTS, dtype=I32)
    dest = jnp.sum(jnp.where(onehot, pstart, 0), axis=-1).astype(I32) + rank
    n_slots = ((t * TOP_K + N_EXPERTS * (bm - 1) + bm - 1) // bm) * bm
    nb = n_slots // bm
    n_used = pend[-1] // bm
    starts = jnp.arange(nb, dtype=I32) * bm
    block_e = jnp.minimum(jnp.sum((pend[None, :] <= starts[:, None]).astype(I32), axis=1), N_EXPERTS - 1)
    block_e = jnp.where(jnp.arange(nb) < n_used, block_e, block_e[jnp.maximum(n_used - 1, 0)]).astype(I32)
    last_blk = jnp.maximum(pend - bm, 0).astype(I32)
    has_blk = (pcounts > 0).astype(I32)
    return dest, block_e, n_used.reshape(1).astype(I32), last_blk, has_blk, n_slots


def kernel(x, norm_mix_g, w_in, gdn_conv_w, gdn_A_log, gdn_dt_bias, gdn_norm_g, w_o_gdn, sconv_w, w_o_sconv,
           w_mix_out, norm_ffn_g, w_router, b_router, w_gate_up, b_gate_up, w_down, b_down, norm_final_g):
    bsz, seq, _ = x.shape
    t = bsz * seq
    assert norm_mix_g.shape[0] == 1, "single-layer block only"
    q, k, v, zs, gbeta, sb, ga, gt = _inproj(x, norm_mix_g[0], w_in[0], gdn_conv_w[0], gdn_A_log[0],
                                             gdn_dt_bias[0], sconv_w[0])
    og = _gdn(q, k, v, zs, gbeta, gdn_norm_g[0])

    def flat(a):
        return a.reshape(t, a.shape[-1])

    h1, xn2, meta, tw, cnt = _mix(flat(og), flat(sb), flat(ga), flat(gt), flat(x), w_o_gdn[0],
                                  w_o_sconv[0], w_mix_out[0], norm_ffn_g[0], w_router[0], b_router[0])
    bm = BM_MOE
    dest, block_e, n_used, last_blk, has_blk, n_slots = _routing_tables(meta, cnt, t, bm)
    xpad = _dispatch(xn2, dest, last_blk, has_blk, n_slots, bm)
    ypad = _moe(xpad, block_e, n_used, w_gate_up[0], b_gate_up[0], w_down[0], b_down[0], bm)
    return _combine(ypad, dest, h1, tw, norm_final_g).reshape(bsz, seq, D_MODEL)
```

```python
import functools

import jax
import jax.numpy as jnp
from jax import lax
from jax.experimental import pallas as pl
from jax.experimental.pallas import tpu as pltpu

F32 = jnp.float32
BF16 = jnp.bfloat16
I32 = jnp.int32

D_MODEL = 1024
HEADS = 8
HEAD_DIM = 128
GDN_WIDTH = HEADS * HEAD_DIM
GDN_CONV = 4
CHUNK = 128
INV_BLOCK = 32
SC_CONV = 3
N_EXPERTS = 32
TOP_K = 4
D_FF = 1024
SWIGLU_LIMIT = 7.0
SWIGLU_ALPHA = 1.702
NORM_EPS = 1e-6

LANES = 128
SUBLANES = 8
VMEM_LIMIT = 56 * 1024 * 1024

TM_IN = 512
CT_IN = 256
TS_GDN = 512
NB_GDN = 2
TM_MIX = 512
MIX_SPLIT = 4
BM_MOE = 512
TM_COMB = 512
ISSUE_UNROLL = 8
CAST_ROWS = 128


NEG_LOG2E = -1.4426950408889634


def _sigmoid(x):
    return 1.0 / (1.0 + jnp.exp2(x * NEG_LOG2E))


def _silu(x):
    return x * _sigmoid(x)


def _softplus(x):
    return jnp.maximum(x, 0.0) + jnp.log(1.0 + jnp.exp(-jnp.abs(x)))


def _rms_norm(x, g):
    return x * lax.rsqrt(jnp.mean(x * x, axis=-1, keepdims=True) + NORM_EPS) * g


def _dot(a, b):
    return jnp.dot(a, b, preferred_element_type=F32)


def _dot_nt(a, b):
    return lax.dot_general(a, b, (((1,), (1,)), ((), ())), preferred_element_type=F32)


def _dot_tn(a, b):
    return lax.dot_general(a, b, (((0,), (0,)), ((), ())), preferred_element_type=F32)


ROW_TILES = D_MODEL // LANES


def _store_row_tiles(ref, val):
    m = val.shape[0]
    for s in range(ROW_TILES):
        ref[pl.ds(s, m, stride=ROW_TILES), :] = val[:, s * LANES:(s + 1) * LANES]


def _load_row_tiles(ref, m, s):
    return ref[pl.ds(s, m, stride=ROW_TILES), :]


def _row_tile(ref, row):
    return ref.at[pl.ds(pl.multiple_of(row * ROW_TILES, ROW_TILES), ROW_TILES), :]


def _causal_conv(p, prev, w, stage_ref):
    taps = w.shape[0]
    tm = p.shape[0]
    stage_ref[0:SUBLANES, :] = prev
    stage_ref[SUBLANES:SUBLANES + tm, :] = p
    y = w[taps - 1:taps] * p
    for s in range(1, taps):
        y = y + w[taps - 1 - s:taps - s] * stage_ref[SUBLANES - s:SUBLANES - s + tm, :]
    return y


def _inproj_kernel(x_ref, ng_ref, wqkv_ref, wz_ref, wab_ref, wxs_ref, wgb_ref, wgc_ref, wga_ref, wgt_ref,
                   cw_ref, scw_ref, alog_ref, dtb_ref,
                   q_ref, k_ref, v_ref, zs_ref, gbeta_ref, sb_ref, ga_ref, gt_ref,
                   carry_qkv, carry_u, xn_scr, stage_ref):
    tm = x_ref.shape[0]

    @pl.when(pl.program_id(1) == 0)
    def _():
        carry_qkv[...] = jnp.zeros_like(carry_qkv)
        carry_u[...] = jnp.zeros_like(carry_u)

    x = x_ref[...]
    xn_scr[...] = _rms_norm(x, ng_ref[...]).astype(BF16)

    q_scale = HEAD_DIM ** -0.5
    outs = (q_ref, k_ref, v_ref)
    for ci in range(3 * GDN_WIDTH // CT_IN):
        c0 = ci * CT_IN
        p = _dot(xn_scr[...], wqkv_ref[:, c0:c0 + CT_IN])
        y = _causal_conv(p, carry_qkv[:, c0:c0 + CT_IN], cw_ref[:, c0:c0 + CT_IN], stage_ref.at[ci % 2])
        carry_qkv[:, c0:c0 + CT_IN] = p[tm - SUBLANES:tm]
        y = _silu(y)
        which = c0 // GDN_WIDTH
        o0 = c0 - which * GDN_WIDTH
        for hh in range(CT_IN // HEAD_DIM):
            yh = y[:, hh * HEAD_DIM:(hh + 1) * HEAD_DIM]
            if which < 2:
                yh = yh * lax.rsqrt(jnp.sum(yh * yh, axis=-1, keepdims=True) + NORM_EPS)
            if which == 0:
                yh = yh * q_scale
            outs[which][:, o0 + hh * HEAD_DIM:o0 + (hh + 1) * HEAD_DIM] = yh.astype(BF16)

    for ci in range(D_MODEL // CT_IN):
        c0 = ci * CT_IN
        zs_ref[:, c0:c0 + CT_IN] = _silu(_dot(xn_scr[...], wz_ref[:, c0:c0 + CT_IN])).astype(BF16)
        ga_ref[:, c0:c0 + CT_IN] = _sigmoid(_dot(xn_scr[...], wga_ref[:, c0:c0 + CT_IN])).astype(BF16)
        gt_ref[:, c0:c0 + CT_IN] = _sigmoid(_dot(xn_scr[...], wgt_ref[:, c0:c0 + CT_IN])).astype(BF16)

    for ci in range(D_MODEL // CT_IN):
        c0 = ci * CT_IN
        xs = _dot(xn_scr[...], wxs_ref[:, c0:c0 + CT_IN])
        gc = _dot(xn_scr[...], wgc_ref[:, c0:c0 + CT_IN])
        u = gc * xs
        y = _causal_conv(u, carry_u[:, c0:c0 + CT_IN], scw_ref[:, c0:c0 + CT_IN], stage_ref.at[ci % 2])
        carry_u[:, c0:c0 + CT_IN] = u[tm - SUBLANES:tm]
        gb = _dot(xn_scr[...], wgb_ref[:, c0:c0 + CT_IN])
        sb_ref[:, c0:c0 + CT_IN] = (gb * y).astype(BF16)

    ab = _dot(xn_scr[...], wab_ref[...])
    lane = lax.broadcasted_iota(I32, ab.shape, 1)
    g = -jnp.exp(alog_ref[...]) * _softplus(ab + dtb_ref[...])
    g_hi = g.astype(BF16).astype(F32)
    gbeta_ref[...] = jnp.where(lane < HEADS, g_hi, jnp.where(lane < 2 * HEADS, _sigmoid(ab), g - g_hi))


def _inproj(x, norm_g, w_in, conv_w, a_log, dt_bias, sconv_w):
    bsz, seq, _ = x.shape
    tm = min(TM_IN, seq)
    o = 0
    w = {}
    for name, width in (("qkv", 3 * GDN_WIDTH), ("z", GDN_WIDTH), ("a", HEADS), ("b", HEADS),
                        ("xs", D_MODEL), ("gb", D_MODEL), ("gc", D_MODEL), ("ga", D_MODEL), ("gt", D_MODEL)):
        w[name] = w_in[:, o:o + width]
        o += width
    wab = (jnp.zeros((D_MODEL, LANES), F32).at[:, :HEADS].set(w["a"]).at[:, HEADS:2 * HEADS].set(w["b"])
           .at[:, 2 * HEADS:3 * HEADS].set(w["a"]))
    alog = jnp.zeros((1, LANES), F32).at[0, :HEADS].set(a_log).at[0, 2 * HEADS:3 * HEADS].set(a_log)
    dtb = jnp.zeros((1, LANES), F32).at[0, :HEADS].set(dt_bias).at[0, 2 * HEADS:3 * HEADS].set(dt_bias)
    weights = [w["qkv"], w["z"], wab, w["xs"], w["gb"], w["gc"], w["ga"], w["gt"]]
    weights = [a.astype(BF16) for a in weights]

    def const(shape):
        return pl.BlockSpec(shape, lambda b, s: (0,) * len(shape), pipeline_mode=pl.Buffered(1))

    def tok(width):
        return pl.BlockSpec((None, tm, width), lambda b, s: (b, s, 0))

    out_bf = jax.ShapeDtypeStruct((bsz, seq, D_MODEL), BF16)
    return pl.pallas_call(
        _inproj_kernel,
        grid=(bsz, seq // tm),
        in_specs=[tok(D_MODEL), const((1, D_MODEL))] + [const(a.shape) for a in weights]
        + [const((GDN_CONV, 3 * GDN_WIDTH)), const((SC_CONV, D_MODEL)), const((1, LANES)), const((1, LANES))],
        out_specs=[tok(D_MODEL)] * 4 + [tok(LANES)] + [tok(D_MODEL)] * 3,
        out_shape=[out_bf] * 4 + [jax.ShapeDtypeStruct((bsz, seq, LANES), F32)] + [out_bf] * 3,
        scratch_shapes=[pltpu.VMEM((SUBLANES, 3 * GDN_WIDTH), F32), pltpu.VMEM((SUBLANES, D_MODEL), F32),
                        pltpu.VMEM((tm, D_MODEL), BF16), pltpu.VMEM((2, SUBLANES + tm, CT_IN), F32)],
        compiler_params=pltpu.CompilerParams(dimension_semantics=("arbitrary", "arbitrary"),
                                             vmem_limit_bytes=VMEM_LIMIT),
        name="inproj",
    )(x, norm_g.reshape(1, D_MODEL), *weights, conv_w, sconv_w, alog, dtb)


def _gdn_kernel(q_ref, k_ref, v_ref, zs_ref, gbeta_ref, ng_ref, o_ref, state_ref):
    nb, ts = q_ref.shape[0], q_ref.shape[1]

    @pl.when(pl.program_id(1) == 0)
    def _():
        state_ref[...] = jnp.zeros_like(state_ref)

    ri = lax.broadcasted_iota(I32, (CHUNK, CHUNK), 0)
    ci = lax.broadcasted_iota(I32, (CHUNK, CHUNK), 1)
    causal = ri >= ci
    strict = ri > ci
    ltri = causal.astype(BF16)
    eye = (ri == ci).astype(F32)
    sh = INV_BLOCK.bit_length() - 1
    diag_blk = (ri >> sh) == (ci >> sh)
    off_blks = []
    while (1 << sh) < CHUNK:
        off_blks.append(((ri >> (sh + 1)) == (ci >> (sh + 1))) & ((ri >> sh) != (ci >> sh)) & strict)
        sh += 1
    ng = ng_ref[...]
    hs = range(nb * HEADS)
    bi = [u // HEADS for u in hs]
    hi = [u % HEADS for u in hs]

    def chunk(c, carry):
        rows = pl.ds(pl.multiple_of(c * CHUNK, CHUNK), CHUNK)
        cols = [slice(hi[h] * HEAD_DIM, (hi[h] + 1) * HEAD_DIM) for h in hs]
        gbc = [gbeta_ref[b, rows, :] for b in range(nb)]
        gcum = [_dot(ltri, gbc[b].astype(BF16)) for b in range(nb)]
        gcum_t = [gcum[b].T for b in range(nb)]
        gc = [gcum[bi[h]][:, hi[h]:hi[h] + 1] + gcum[bi[h]][:, 2 * HEADS + hi[h]:2 * HEADS + hi[h] + 1]
              for h in hs]
        gr = [gcum_t[bi[h]][hi[h]:hi[h] + 1, :] + gcum_t[bi[h]][2 * HEADS + hi[h]:2 * HEADS + hi[h] + 1, :]
              for h in hs]
        beta = [gbc[bi[h]][:, HEADS + hi[h]:HEADS + hi[h] + 1] for h in hs]
        g_last = [gc[h][CHUNK - 1:CHUNK, :] for h in hs]
        decay = [jnp.where(causal, jnp.exp(jnp.where(causal, gc[h] - gr[h], 0.0)), 0.0) for h in hs]
        eg = [jnp.exp(gc[h]) for h in hs]

        kf = [k_ref[bi[h], rows, cols[h]].astype(F32) for h in hs]
        kb = [kf[h] * beta[h] for h in hs]
        kt = [kf[h].T for h in hs]
        ktb = [kt[h].astype(BF16) for h in hs]
        lhs = [jnp.concatenate([kb[h].astype(BF16), q_ref[bi[h], rows, cols[h]]], axis=0) for h in hs]
        aq = [_dot(lhs[h], ktb[h]) for h in hs]
        aqk = [(aq[h][CHUNK:] * decay[h]).astype(BF16) for h in hs]
        lm = [jnp.where(strict, aq[h][:CHUNK] * decay[h], 0.0) for h in hs]
        m = [jnp.where(diag_blk, -lm[h], 0.0) for h in hs]
        tinv = [eye + m[h] for h in hs]
        mb = [m[h].astype(BF16) for h in hs]
        m = [_dot(mb[h], mb[h]) for h in hs]
        levels = INV_BLOCK.bit_length() - 1
        for j in range(1, levels):
            mb = [m[h].astype(BF16) for h in hs]
            if j < levels - 1:
                st = [_dot(jnp.concatenate([tinv[h].astype(BF16), mb[h]], axis=0), mb[h]) for h in hs]
                tinv = [tinv[h] + st[h][:CHUNK] for h in hs]
                m = [st[h][CHUNK:] for h in hs]
            else:
                tinv = [tinv[h] + _dot(tinv[h].astype(BF16), mb[h]) for h in hs]
        for off in off_blks:
            tb = [tinv[h].astype(BF16) for h in hs]
            ct = [_dot(jnp.where(off, lm[h], 0.0).astype(BF16), tb[h]) for h in hs]
            tinv = [tinv[h] - _dot(tb[h], ct[h].astype(BF16)) for h in hs]
        rhs = [jnp.concatenate([(v_ref[bi[h], rows, cols[h]].astype(F32) * beta[h]).astype(BF16),
                                (kb[h] * eg[h]).astype(BF16)], axis=1) for h in hs]
        uw = [_dot(tinv[h].astype(BF16), rhs[h]) for h in hs]

        sts = [state_ref[h] for h in hs]
        stb = [sts[h].astype(BF16) for h in hs]
        wq = [jnp.concatenate([uw[h][:, HEAD_DIM:].astype(BF16),
                               (q_ref[bi[h], rows, cols[h]].astype(F32) * eg[h]).astype(BF16)], axis=0) for h in hs]
        ws = [_dot(wq[h], stb[h]) for h in hs]
        vnb = [(uw[h][:, :HEAD_DIM] - ws[h][:CHUNK]).astype(BF16) for h in hs]
        kdt = [(kt[h] * jnp.exp(g_last[h] - gr[h])).astype(BF16) for h in hs]
        o = [ws[h][CHUNK:] + _dot(aqk[h], vnb[h]) for h in hs]
        for h in hs:
            state_ref[h] = sts[h] * jnp.exp(g_last[h]) + _dot(kdt[h], vnb[h])
        for h in hs:
            oh = _rms_norm(o[h], ng) * zs_ref[bi[h], rows, cols[h]].astype(F32)
            o_ref[bi[h], rows, cols[h]] = oh.astype(BF16)
        return carry

    lax.fori_loop(0, ts // CHUNK, chunk, 0)


def _gdn(q, k, v, zs, gbeta, norm_g):
    bsz, seq, _ = q.shape
    ts = min(TS_GDN, seq)
    nb = NB_GDN if bsz % NB_GDN == 0 else 1

    def tok(width):
        return pl.BlockSpec((nb, ts, width), lambda b, s: (b, s, 0))

    return pl.pallas_call(
        _gdn_kernel,
        grid=(bsz // nb, seq // ts),
        in_specs=[tok(GDN_WIDTH)] * 4 + [tok(LANES), pl.BlockSpec((1, HEAD_DIM), lambda b, s: (0, 0))],
        out_specs=tok(GDN_WIDTH),
        out_shape=jax.ShapeDtypeStruct((bsz, seq, GDN_WIDTH), BF16),
        scratch_shapes=[pltpu.VMEM((nb * HEADS, HEAD_DIM, HEAD_DIM), F32)],
        compiler_params=pltpu.CompilerParams(dimension_semantics=("arbitrary", "arbitrary"),
                                             vmem_limit_bytes=VMEM_LIMIT),
        name="gdn",
    )(q, k, v, zs, gbeta, norm_g.reshape(1, HEAD_DIM))


def _mix_kernel(og_ref, sb_ref, ga_ref, gt_ref, x_ref, wog_ref, wos_ref, wmix_ref, gffn_ref, wr_ref, br_ref,
                h1_ref, xn2_ref, meta_ref, tw_ref, cnt_ref, carry_ref):
    tm = x_ref.shape[0]

    @pl.when(pl.program_id(0) == 0)
    def _():
        carry_ref[...] = jnp.zeros_like(carry_ref)

    sub = tm // MIX_SPLIT
    rs = [slice(r * sub, (r + 1) * sub) for r in range(MIX_SPLIT)]
    ya = [_dot(og_ref[r, :], wog_ref[...]) for r in rs]
    yb = [_dot(sb_ref[r, :], wos_ref[...]) for r in rs]
    merged = [(ga_ref[r, :].astype(F32) * a + gt_ref[r, :].astype(F32) * b).astype(BF16)
              for r, a, b in zip(rs, ya, yb)]
    h1 = [x_ref[r, :] + _dot(mg, wmix_ref[...]) for r, mg in zip(rs, merged)]
    xn2 = [_rms_norm(h, gffn_ref[...]) for h in h1]
    x_hi = [v.astype(BF16) for v in xn2]
    x_lo = [(v - hi.astype(F32)).astype(BF16) for v, hi in zip(xn2, x_hi)]
    parts = [_dot(hi, wr_ref[...]) + _dot(lo, wr_ref[...]) for hi, lo in zip(x_hi, x_lo)]
    for r, h in zip(rs, h1):
        h1_ref[r, :] = h
    _store_row_tiles(xn2_ref, jnp.concatenate(xn2, axis=0))
    logits = jnp.concatenate([p[:, :LANES] + p[:, LANES:] for p in parts], axis=0) + br_ref[...]

    lane = lax.broadcasted_iota(I32, (tm, LANES), 1)
    lane_f = lane.astype(F32)
    neg_inf = jnp.float32(-jnp.inf)
    work = jnp.where(lane < N_EXPERTS, logits, neg_inf)
    hits, vals = [], []
    for _ in range(TOP_K):
        m = jnp.max(work, axis=-1, keepdims=True)
        first = jnp.min(jnp.where(work == m, lane_f, float(LANES)), axis=-1, keepdims=True)
        hit = lane_f == first
        work = jnp.where(hit, neg_inf, work)
        hits.append(hit)
        vals.append((m, first))
    exps = [jnp.exp(m - vals[0][0]) for m, _ in vals]
    denom = exps[0] + exps[1] + exps[2] + exps[3]

    sel = jnp.zeros((tm, LANES), F32)
    for hit in hits:
        sel = sel + hit.astype(F32)
    ri = lax.broadcasted_iota(I32, (tm, tm), 0)
    ci = lax.broadcasted_iota(I32, (tm, tm), 1)
    before = (ri > ci).astype(BF16)
    rank_mat = _dot(before, sel.astype(BF16)) + carry_ref[...]
    carry_ref[...] = carry_ref[...] + jnp.sum(sel, axis=0, keepdims=True)
    cnt_ref[...] = carry_ref[...]

    meta = jnp.zeros((tm, LANES), F32)
    tw_out = jnp.zeros((tm, LANES), F32)
    for kk in range(TOP_K):
        rk = jnp.sum(jnp.where(hits[kk], rank_mat, 0.0), axis=-1, keepdims=True)
        meta = jnp.where(lane == kk, vals[kk][1], meta)
        meta = jnp.where(lane == TOP_K + kk, rk, meta)
        tw_out = jnp.where(lane == kk, exps[kk] / denom, tw_out)
    meta_ref[0] = meta.T[0:2 * TOP_K, :].astype(I32)
    tw_ref[...] = tw_out


def _mix(og, sb, ga, gt, x, w_o_gdn, w_o_sconv, w_mix_out, norm_ffn_g, w_router, b_router):
    t = x.shape[0]
    tm = min(TM_MIX, t)
    nt = t // tm
    wr = jnp.zeros((D_MODEL, LANES), F32).at[:, :N_EXPERTS].set(w_router)
    wr_hi = wr.astype(BF16)
    wr = jnp.concatenate([wr_hi, (wr - wr_hi.astype(F32)).astype(BF16)], axis=1)
    br = jnp.zeros((1, LANES), F32).at[0, :N_EXPERTS].set(b_router)

    def const(shape):
        return pl.BlockSpec(shape, lambda i: (0,) * len(shape), pipeline_mode=pl.Buffered(1))

    def tok(rows, width):
        return pl.BlockSpec((rows, width), lambda i: (i, 0))

    return pl.pallas_call(
        _mix_kernel,
        grid=(nt,),
        in_specs=[tok(tm, D_MODEL)] * 5 + [const((D_MODEL, D_MODEL))] * 3
        + [const((1, D_MODEL)), const((D_MODEL, 2 * LANES)), const((1, LANES))],
        out_specs=[tok(tm, D_MODEL), tok(tm * ROW_TILES, LANES),
                   pl.BlockSpec((1, 2 * TOP_K, tm), lambda i: (i, 0, 0)), tok(tm, LANES),
                   pl.BlockSpec((1, LANES), lambda i: (0, 0))],
        out_shape=[jax.ShapeDtypeStruct((t, D_MODEL), F32), jax.ShapeDtypeStruct((t * ROW_TILES, LANES), F32),
                   jax.ShapeDtypeStruct((nt, 2 * TOP_K, tm), I32), jax.ShapeDtypeStruct((t, LANES), F32),
                   jax.ShapeDtypeStruct((1, LANES), F32)],
        scratch_shapes=[pltpu.VMEM((1, LANES), F32)],
        compiler_params=pltpu.CompilerParams(dimension_semantics=("arbitrary",), vmem_limit_bytes=VMEM_LIMIT),
        name="mix_router",
    )(og, sb, ga, gt, x, w_o_gdn.astype(BF16), w_o_sconv.astype(BF16), w_mix_out.astype(BF16),
      norm_ffn_g.reshape(1, D_MODEL), wr, br)


def _dispatch_kernel(last_ref, has_ref, dest_ref, x_ref, xpad_ref, zero_ref, sem_z, sem):
    bm = zero_ref.shape[0] // ROW_TILES
    tm = x_ref.shape[0] // ROW_TILES

    @pl.when(pl.program_id(0) == 0)
    def _():
        zero_ref[...] = jnp.zeros_like(zero_ref)
        for e in range(N_EXPERTS):
            @pl.when(has_ref[e] > 0)
            def _():
                start = pl.multiple_of(last_ref[e] * ROW_TILES, bm * ROW_TILES)
                cp = pltpu.make_async_copy(zero_ref, xpad_ref.at[pl.ds(start, bm * ROW_TILES), :], sem_z)
                cp.start()
                cp.wait()

    def issue(g, carry):
        for u in range(ISSUE_UNROLL):
            tok = g * (ISSUE_UNROLL // TOP_K) + u // TOP_K
            pltpu.make_async_copy(_row_tile(x_ref, tok),
                                  _row_tile(xpad_ref, dest_ref[0, 0, g * ISSUE_UNROLL + u]), sem).start()
        return carry

    lax.fori_loop(0, tm * TOP_K // ISSUE_UNROLL, issue, 0)
    for _ in range(TOP_K):
        pltpu.make_async_copy(x_ref, xpad_ref.at[pl.ds(0, tm * ROW_TILES), :], sem).wait()


def _dispatch(xn2_tiles, dest, last_blk, has_blk, n_slots, bm):
    t = xn2_tiles.shape[0] // ROW_TILES
    tm = min(TM_MIX, t)
    nt = t // tm
    grid_spec = pltpu.PrefetchScalarGridSpec(
        num_scalar_prefetch=2,
        grid=(nt,),
        in_specs=[pl.BlockSpec((1, 1, tm * TOP_K), lambda i, a, b: (i, 0, 0), memory_space=pltpu.SMEM),
                  pl.BlockSpec((tm * ROW_TILES, LANES), lambda i, a, b: (i, 0))],
        out_specs=pl.BlockSpec(memory_space=pl.ANY),
        scratch_shapes=[pltpu.VMEM((bm * ROW_TILES, LANES), F32), pltpu.SemaphoreType.DMA(()),
                        pltpu.SemaphoreType.DMA(())],
    )
    return pl.pallas_call(
        _dispatch_kernel,
        grid_spec=grid_spec,
        out_shape=jax.ShapeDtypeStruct((n_slots * ROW_TILES, LANES), F32),
        compiler_params=pltpu.CompilerParams(dimension_semantics=("arbitrary",), vmem_limit_bytes=VMEM_LIMIT,
                                             has_side_effects=True, disable_bounds_checks=True),
        name="dispatch",
    )(last_blk, has_blk, dest.reshape(nt, 1, tm * TOP_K), xn2_tiles)


def _moe_kernel(be_ref, nu_ref, x_ref, wgu_ref, bgu_ref, wd_ref, bd_ref, y_ref, wgu_bf, wd_bf):
    bm = x_ref.shape[0] // ROW_TILES
    i = pl.program_id(0)

    @pl.when(i < nu_ref[0])
    def _():
        @pl.when((i == 0) | (be_ref[i] != be_ref[jnp.maximum(i - 1, 0)]))
        def _():
            def cast_rows(r, carry):
                rows = pl.ds(pl.multiple_of(r * CAST_ROWS, CAST_ROWS), CAST_ROWS)
                wgu_bf[rows, :] = wgu_ref[rows, :].astype(BF16)
                wd_bf[rows, :] = wd_ref[rows, :].astype(BF16)
                return carry

            lax.fori_loop(0, D_MODEL // CAST_ROWS, cast_rows, 0)

        xb = jnp.concatenate([_load_row_tiles(x_ref, bm, s).astype(BF16) for s in range(ROW_TILES)], axis=1)
        gu = _dot(xb, wgu_bf[...]) + bgu_ref[...]
        gate = jnp.minimum(gu[:, :D_FF], SWIGLU_LIMIT)
        up = jnp.clip(gu[:, D_FF:], -SWIGLU_LIMIT, SWIGLU_LIMIT)
        hid = (up + 1.0) * gate * _sigmoid(SWIGLU_ALPHA * gate)
        _store_row_tiles(y_ref, _dot(hid.astype(BF16), wd_bf[...]) + bd_ref[...])


def _moe(xpad, block_e, n_used, w_gate_up, b_gate_up, w_down, b_down, bm):
    nb = xpad.shape[0] // (bm * ROW_TILES)

    def rows(i, be, nu):
        return (jnp.minimum(i, nu[0] - 1), 0)

    def expert(i, be, nu):
        return (be[i], 0, 0)

    grid_spec = pltpu.PrefetchScalarGridSpec(
        num_scalar_prefetch=2,
        grid=(nb,),
        in_specs=[pl.BlockSpec((bm * ROW_TILES, LANES), rows),
                  pl.BlockSpec((None, D_MODEL, 2 * D_FF), expert),
                  pl.BlockSpec((None, 1, 2 * D_FF), expert),
                  pl.BlockSpec((None, D_FF, D_MODEL), expert),
                  pl.BlockSpec((None, 1, D_MODEL), expert)],
        out_specs=pl.BlockSpec((bm * ROW_TILES, LANES), rows),
        scratch_shapes=[pltpu.VMEM((D_MODEL, 2 * D_FF), BF16), pltpu.VMEM((D_FF, D_MODEL), BF16)],
    )
    return pl.pallas_call(
        _moe_kernel,
        grid_spec=grid_spec,
        out_shape=jax.ShapeDtypeStruct(xpad.shape, F32),
        compiler_params=pltpu.CompilerParams(dimension_semantics=("arbitrary",), vmem_limit_bytes=VMEM_LIMIT),
        name="moe_mlp",
    )(block_e, n_used, xpad, w_gate_up, b_gate_up.reshape(N_EXPERTS, 1, 2 * D_FF),
      w_down, b_down.reshape(N_EXPERTS, 1, D_MODEL))


def _combine_kernel(dest_ref, dest_next_ref, h1_ref, tw_ref, gf_ref, ypad_ref, out_ref, buf_ref, sem):
    tm = h1_ref.shape[0]
    i = pl.program_id(0)
    cur = i % 2

    def gather(d_ref, parity):
        def issue(g, carry):
            for u in range(ISSUE_UNROLL):
                tok = g * (ISSUE_UNROLL // TOP_K) + u // TOP_K
                pltpu.make_async_copy(_row_tile(ypad_ref, d_ref[0, 0, g * ISSUE_UNROLL + u]),
                                      _row_tile(buf_ref.at[parity * TOP_K + u % TOP_K], tok),
                                      sem.at[parity]).start()
            return carry

        lax.fori_loop(0, tm * TOP_K // ISSUE_UNROLL, issue, 0)

    @pl.when(i == 0)
    def _():
        gather(dest_ref, 0)

    @pl.when(i + 1 < pl.num_programs(0))
    def _():
        gather(dest_next_ref, 1 - cur)

    for kk in range(TOP_K):
        pltpu.make_async_copy(ypad_ref.at[pl.ds(0, tm * ROW_TILES), :], buf_ref.at[cur * TOP_K + kk],
                              sem.at[cur]).wait()

    tw = tw_ref[...]
    accs = []
    ssq = jnp.zeros((tm, 1), F32)
    for s in range(ROW_TILES):
        acc = h1_ref[:, s * LANES:(s + 1) * LANES]
        for kk in range(TOP_K):
            acc = acc + tw[:, kk:kk + 1] * _load_row_tiles(buf_ref.at[cur * TOP_K + kk], tm, s)
        ssq = ssq + jnp.sum(acc * acc, axis=-1, keepdims=True)
        accs.append(acc)
    inv = lax.rsqrt(ssq * (1.0 / D_MODEL) + NORM_EPS)
    for s in range(ROW_TILES):
        out_ref[:, s * LANES:(s + 1) * LANES] = accs[s] * inv * gf_ref[:, s * LANES:(s + 1) * LANES]


def _combine(ypad, dest, h1, tw, norm_final_g):
    t = h1.shape[0]
    tm = min(TM_COMB, t)
    nt = t // tm
    dest3 = dest.reshape(nt, 1, tm * TOP_K)
    return pl.pallas_call(
        _combine_kernel,
        grid=(nt,),
        in_specs=[pl.BlockSpec((1, 1, tm * TOP_K), lambda i: (i, 0, 0), memory_space=pltpu.SMEM),
                  pl.BlockSpec((1, 1, tm * TOP_K), lambda i: (jnp.minimum(i + 1, nt - 1), 0, 0),
                               memory_space=pltpu.SMEM),
                  pl.BlockSpec((tm, D_MODEL), lambda i: (i, 0)),
                  pl.BlockSpec((tm, LANES), lambda i: (i, 0)),
                  pl.BlockSpec((1, D_MODEL), lambda i: (0, 0)),
                  pl.BlockSpec(memory_space=pl.ANY)],
        out_specs=pl.BlockSpec((tm, D_MODEL), lambda i: (i, 0)),
        out_shape=jax.ShapeDtypeStruct((t, D_MODEL), F32),
        scratch_shapes=[pltpu.VMEM((2 * TOP_K, tm * ROW_TILES, LANES), F32), pltpu.SemaphoreType.DMA((2,))],
        compiler_params=pltpu.CompilerParams(dimension_semantics=("arbitrary",), vmem_limit_bytes=VMEM_LIMIT,
                                             disable_bounds_checks=True),
        name="combine",
    )(dest3, dest3, h1, tw, norm_final_g.reshape(1, D_MODEL), ypad)


def _routing_tables(meta, cnt, t, bm):
    counts = cnt[0, :N_EXPERTS].astype(I32)
    pcounts = ((counts + bm - 1) // bm) * bm
    pend = jnp.cumsum(pcounts)
    pstart = (pend - pcounts).astype(I32)
    idx, rank = meta[:, :TOP_K, :], meta[:, TOP_K:, :]
    onehot = idx[..., None] == jnp.arange(N_EXPERTS, dtype=I32)
    dest = jnp.sum(jnp.where(onehot, pstart, 0), axis=-1).astype(I32) + rank
    dest = dest.transpose(0, 2, 1).reshape(-1)
    n_slots = ((t * TOP_K + N_EXPERTS * (bm - 1) + bm - 1) // bm) * bm
    nb = n_slots // bm
    n_used = pend[-1] // bm
    starts = jnp.arange(nb, dtype=I32) * bm
    block_e = jnp.minimum(jnp.sum((pend[None, :] <= starts[:, None]).astype(I32), axis=1), N_EXPERTS - 1)
    block_e = jnp.where(jnp.arange(nb) < n_used, block_e, block_e[jnp.maximum(n_used - 1, 0)]).astype(I32)
    last_blk = jnp.maximum(pend - bm, 0).astype(I32)
    has_blk = (pcounts > 0).astype(I32)
    return dest, block_e, n_used.reshape(1).astype(I32), last_blk, has_blk, n_slots


def kernel(x, norm_mix_g, w_in, gdn_conv_w, gdn_A_log, gdn_dt_bias, gdn_norm_g, w_o_gdn, sconv_w, w_o_sconv,
           w_mix_out, norm_ffn_g, w_router, b_router, w_gate_up, b_gate_up, w_down, b_down, norm_final_g):
    bsz, seq, _ = x.shape
    t = bsz * seq
    assert norm_mix_g.shape[0] == 1, "single-layer block only"
    q, k, v, zs, gbeta, sb, ga, gt = _inproj(x, norm_mix_g[0], w_in[0], gdn_conv_w[0], gdn_A_log[0],
                                             gdn_dt_bias[0], sconv_w[0])
    og = _gdn(q, k, v, zs, gbeta, gdn_norm_g[0])

    def flat(a):
        return a.reshape(t, a.shape[-1])

    h1, xn2, meta, tw, cnt = _mix(flat(og), flat(sb), flat(ga), flat(gt), flat(x), w_o_gdn[0],
                                  w_o_sconv[0], w_mix_out[0], norm_ffn_g[0], w_router[0], b_router[0])
    bm = BM_MOE
    dest, block_e, n_used, last_blk, has_blk, n_slots = _routing_tables(meta, cnt, t, bm)
    xpad = _dispatch(xn2, dest, last_blk, has_blk, n_slots, bm)
    ypad = _moe(xpad, block_e, n_used, w_gate_up[0], b_gate_up[0], w_down[0], b_down[0], bm)
    return _combine(ypad, dest, h1, tw, norm_final_g).reshape(bsz, seq, D_MODEL)
```

```python
import functools

import jax
import jax.numpy as jnp
from jax import lax
from jax.experimental import pallas as pl
from jax.experimental.pallas import tpu as pltpu

F32 = jnp.float32
BF16 = jnp.bfloat16
I32 = jnp.int32

D_MODEL = 1024
HEADS = 8
HEAD_DIM = 128
GDN_WIDTH = HEADS * HEAD_DIM
GDN_CONV = 4
CHUNK = 128
INV_BLOCK = 32
SC_CONV = 3
N_EXPERTS = 32
TOP_K = 4
D_FF = 1024
SWIGLU_LIMIT = 7.0
SWIGLU_ALPHA = 1.702
NORM_EPS = 1e-6

LANES = 128
SUBLANES = 8
VMEM_LIMIT = 56 * 1024 * 1024

TM_IN = 512
CT_IN = 256
TS_GDN = 512
NB_GDN = 2
TM_MIX = 512
MIX_SPLIT = 4
BM_MOE = 512
TM_COMB = 256
ISSUE_UNROLL = 8
CAST_ROWS = 128


NEG_LOG2E = -1.4426950408889634


def _sigmoid(x):
    return 1.0 / (1.0 + jnp.exp2(x * NEG_LOG2E))


def _silu(x):
    return x * _sigmoid(x)


def _softplus(x):
    return jnp.maximum(x, 0.0) + jnp.log(1.0 + jnp.exp(-jnp.abs(x)))


def _rms_norm(x, g):
    return x * lax.rsqrt(jnp.mean(x * x, axis=-1, keepdims=True) + NORM_EPS) * g


def _dot(a, b):
    return jnp.dot(a, b, preferred_element_type=F32)


def _dot_nt(a, b):
    return lax.dot_general(a, b, (((1,), (1,)), ((), ())), preferred_element_type=F32)


def _dot_tn(a, b):
    return lax.dot_general(a, b, (((0,), (0,)), ((), ())), preferred_element_type=F32)


ROW_TILES = D_MODEL // LANES


def _store_row_tiles(ref, val):
    m = val.shape[0]
    for s in range(ROW_TILES):
        ref[pl.ds(s, m, stride=ROW_TILES), :] = val[:, s * LANES:(s + 1) * LANES]


def _load_row_tiles(ref, m, s):
    return ref[pl.ds(s, m, stride=ROW_TILES), :]


def _row_tile(ref, row):
    return ref.at[pl.ds(pl.multiple_of(row * ROW_TILES, ROW_TILES), ROW_TILES), :]


def _causal_conv(p, prev, w, stage_ref):
    taps = w.shape[0]
    tm = p.shape[0]
    stage_ref[0:SUBLANES, :] = prev
    stage_ref[SUBLANES:SUBLANES + tm, :] = p
    y = w[taps - 1:taps] * p
    for s in range(1, taps):
        y = y + w[taps - 1 - s:taps - s] * stage_ref[SUBLANES - s:SUBLANES - s + tm, :]
    return y


def _inproj_kernel(x_ref, ng_ref, wqkv_ref, wz_ref, wab_ref, wxs_ref, wgb_ref, wgc_ref, wga_ref, wgt_ref,
                   cw_ref, scw_ref, alog_ref, dtb_ref,
                   q_ref, k_ref, v_ref, zs_ref, gbeta_ref, sb_ref, ga_ref, gt_ref,
                   carry_qkv, carry_u, xn_scr, stage_ref):
    tm = x_ref.shape[0]

    @pl.when(pl.program_id(1) == 0)
    def _():
        carry_qkv[...] = jnp.zeros_like(carry_qkv)
        carry_u[...] = jnp.zeros_like(carry_u)

    x = x_ref[...]
    xn_scr[...] = _rms_norm(x, ng_ref[...]).astype(BF16)

    q_scale = HEAD_DIM ** -0.5
    outs = (q_ref, k_ref, v_ref)
    for ci in range(3 * GDN_WIDTH // CT_IN):
        c0 = ci * CT_IN
        p = _dot(xn_scr[...], wqkv_ref[:, c0:c0 + CT_IN])
        y = _causal_conv(p, carry_qkv[:, c0:c0 + CT_IN], cw_ref[:, c0:c0 + CT_IN], stage_ref.at[ci % 2])
        carry_qkv[:, c0:c0 + CT_IN] = p[tm - SUBLANES:tm]
        y = _silu(y)
        which = c0 // GDN_WIDTH
        o0 = c0 - which * GDN_WIDTH
        for hh in range(CT_IN // HEAD_DIM):
            yh = y[:, hh * HEAD_DIM:(hh + 1) * HEAD_DIM]
            if which < 2:
                yh = yh * lax.rsqrt(jnp.sum(yh * yh, axis=-1, keepdims=True) + NORM_EPS)
            if which == 0:
                yh = yh * q_scale
            outs[which][:, o0 + hh * HEAD_DIM:o0 + (hh + 1) * HEAD_DIM] = yh.astype(BF16)

    for ci in range(D_MODEL // CT_IN):
        c0 = ci * CT_IN
        zs_ref[:, c0:c0 + CT_IN] = _silu(_dot(xn_scr[...], wz_ref[:, c0:c0 + CT_IN])).astype(BF16)
        ga_ref[:, c0:c0 + CT_IN] = _sigmoid(_dot(xn_scr[...], wga_ref[:, c0:c0 + CT_IN])).astype(BF16)
        gt_ref[:, c0:c0 + CT_IN] = _sigmoid(_dot(xn_scr[...], wgt_ref[:, c0:c0 + CT_IN])).astype(BF16)

    for ci in range(D_MODEL // CT_IN):
        c0 = ci * CT_IN
        xs = _dot(xn_scr[...], wxs_ref[:, c0:c0 + CT_IN])
        gc = _dot(xn_scr[...], wgc_ref[:, c0:c0 + CT_IN])
        u = gc * xs
        y = _causal_conv(u, carry_u[:, c0:c0 + CT_IN], scw_ref[:, c0:c0 + CT_IN], stage_ref.at[ci % 2])
        carry_u[:, c0:c0 + CT_IN] = u[tm - SUBLANES:tm]
        gb = _dot(xn_scr[...], wgb_ref[:, c0:c0 + CT_IN])
        sb_ref[:, c0:c0 + CT_IN] = (gb * y).astype(BF16)

    ab = _dot(xn_scr[...], wab_ref[...])
    lane = lax.broadcasted_iota(I32, ab.shape, 1)
    g = -jnp.exp(alog_ref[...]) * _softplus(ab + dtb_ref[...])
    g_hi = g.astype(BF16).astype(F32)
    gbeta_ref[...] = jnp.where(lane < HEADS, g_hi, jnp.where(lane < 2 * HEADS, _sigmoid(ab), g - g_hi))


def _inproj(x, norm_g, w_in, conv_w, a_log, dt_bias, sconv_w):
    bsz, seq, _ = x.shape
    tm = min(TM_IN, seq)
    o = 0
    w = {}
    for name, width in (("qkv", 3 * GDN_WIDTH), ("z", GDN_WIDTH), ("a", HEADS), ("b", HEADS),
                        ("xs", D_MODEL), ("gb", D_MODEL), ("gc", D_MODEL), ("ga", D_MODEL), ("gt", D_MODEL)):
        w[name] = w_in[:, o:o + width]
        o += width
    wab = (jnp.zeros((D_MODEL, LANES), F32).at[:, :HEADS].set(w["a"]).at[:, HEADS:2 * HEADS].set(w["b"])
           .at[:, 2 * HEADS:3 * HEADS].set(w["a"]))
    alog = jnp.zeros((1, LANES), F32).at[0, :HEADS].set(a_log).at[0, 2 * HEADS:3 * HEADS].set(a_log)
    dtb = jnp.zeros((1, LANES), F32).at[0, :HEADS].set(dt_bias).at[0, 2 * HEADS:3 * HEADS].set(dt_bias)
    weights = [w["qkv"], w["z"], wab, w["xs"], w["gb"], w["gc"], w["ga"], w["gt"]]
    weights = [a.astype(BF16) for a in weights]

    def const(shape):
        return pl.BlockSpec(shape, lambda b, s: (0,) * len(shape), pipeline_mode=pl.Buffered(1))

    def tok(width):
        return pl.BlockSpec((None, tm, width), lambda b, s: (b, s, 0))

    out_bf = jax.ShapeDtypeStruct((bsz, seq, D_MODEL), BF16)
    return pl.pallas_call(
        _inproj_kernel,
        grid=(bsz, seq // tm),
        in_specs=[tok(D_MODEL), const((1, D_MODEL))] + [const(a.shape) for a in weights]
        + [const((GDN_CONV, 3 * GDN_WIDTH)), const((SC_CONV, D_MODEL)), const((1, LANES)), const((1, LANES))],
        out_specs=[tok(D_MODEL)] * 4 + [tok(LANES)] + [tok(D_MODEL)] * 3,
        out_shape=[out_bf] * 4 + [jax.ShapeDtypeStruct((bsz, seq, LANES), F32)] + [out_bf] * 3,
        scratch_shapes=[pltpu.VMEM((SUBLANES, 3 * GDN_WIDTH), F32), pltpu.VMEM((SUBLANES, D_MODEL), F32),
                        pltpu.VMEM((tm, D_MODEL), BF16), pltpu.VMEM((2, SUBLANES + tm, CT_IN), F32)],
        compiler_params=pltpu.CompilerParams(dimension_semantics=("arbitrary", "arbitrary"),
                                             vmem_limit_bytes=VMEM_LIMIT),
        name="inproj",
    )(x, norm_g.reshape(1, D_MODEL), *weights, conv_w, sconv_w, alog, dtb)


def _gdn_kernel(q_ref, k_ref, v_ref, zs_ref, gbeta_ref, ng_ref, o_ref, state_ref):
    nb, ts = q_ref.shape[0], q_ref.shape[1]

    @pl.when(pl.program_id(1) == 0)
    def _():
        state_ref[...] = jnp.zeros_like(state_ref)

    ri = lax.broadcasted_iota(I32, (CHUNK, CHUNK), 0)
    ci = lax.broadcasted_iota(I32, (CHUNK, CHUNK), 1)
    causal = ri >= ci
    strict = ri > ci
    ltri = causal.astype(BF16)
    eye = (ri == ci).astype(F32)
    sh = INV_BLOCK.bit_length() - 1
    diag_blk = (ri >> sh) == (ci >> sh)
    off_blks = []
    while (1 << sh) < CHUNK:
        off_blks.append(((ri >> (sh + 1)) == (ci >> (sh + 1))) & ((ri >> sh) != (ci >> sh)) & strict)
        sh += 1
    ng = ng_ref[...]
    hs = range(nb * HEADS)
    bi = [u // HEADS for u in hs]
    hi = [u % HEADS for u in hs]

    def chunk(c, carry):
        rows = pl.ds(pl.multiple_of(c * CHUNK, CHUNK), CHUNK)
        cols = [slice(hi[h] * HEAD_DIM, (hi[h] + 1) * HEAD_DIM) for h in hs]
        gbc = [gbeta_ref[b, rows, :] for b in range(nb)]
        gcum = [_dot(ltri, gbc[b].astype(BF16)) for b in range(nb)]
        gcum_t = [gcum[b].T for b in range(nb)]
        gc = [gcum[bi[h]][:, hi[h]:hi[h] + 1] + gcum[bi[h]][:, 2 * HEADS + hi[h]:2 * HEADS + hi[h] + 1]
              for h in hs]
        gr = [gcum_t[bi[h]][hi[h]:hi[h] + 1, :] + gcum_t[bi[h]][2 * HEADS + hi[h]:2 * HEADS + hi[h] + 1, :]
              for h in hs]
        beta = [gbc[bi[h]][:, HEADS + hi[h]:HEADS + hi[h] + 1] for h in hs]
        g_last = [gc[h][CHUNK - 1:CHUNK, :] for h in hs]
        decay = [jnp.where(causal, jnp.exp(jnp.where(causal, gc[h] - gr[h], 0.0)), 0.0) for h in hs]
        eg = [jnp.exp(gc[h]) for h in hs]

        kf = [k_ref[bi[h], rows, cols[h]].astype(F32) for h in hs]
        kb = [kf[h] * beta[h] for h in hs]
        kt = [kf[h].T for h in hs]
        ktb = [kt[h].astype(BF16) for h in hs]
        lhs = [jnp.concatenate([kb[h].astype(BF16), q_ref[bi[h], rows, cols[h]]], axis=0) for h in hs]
        aq = [_dot(lhs[h], ktb[h]) for h in hs]
        aqk = [(aq[h][CHUNK:] * decay[h]).astype(BF16) for h in hs]
        lm = [jnp.where(strict, aq[h][:CHUNK] * decay[h], 0.0) for h in hs]
        m = [jnp.where(diag_blk, -lm[h], 0.0) for h in hs]
        tinv = [eye + m[h] for h in hs]
        mb = [m[h].astype(BF16) for h in hs]
        m = [_dot(mb[h], mb[h]) for h in hs]
        levels = INV_BLOCK.bit_length() - 1
        for j in range(1, levels):
            mb = [m[h].astype(BF16) for h in hs]
            if j < levels - 1:
                st = [_dot(jnp.concatenate([tinv[h].astype(BF16), mb[h]], axis=0), mb[h]) for h in hs]
                tinv = [tinv[h] + st[h][:CHUNK] for h in hs]
                m = [st[h][CHUNK:] for h in hs]
            else:
                tinv = [tinv[h] + _dot(tinv[h].astype(BF16), mb[h]) for h in hs]
        for off in off_blks:
            tb = [tinv[h].astype(BF16) for h in hs]
            ct = [_dot(jnp.where(off, lm[h], 0.0).astype(BF16), tb[h]) for h in hs]
            tinv = [tinv[h] - _dot(tb[h], ct[h].astype(BF16)) for h in hs]
        rhs = [jnp.concatenate([(v_ref[bi[h], rows, cols[h]].astype(F32) * beta[h]).astype(BF16),
                                (kb[h] * eg[h]).astype(BF16)], axis=1) for h in hs]
        uw = [_dot(tinv[h].astype(BF16), rhs[h]) for h in hs]

        sts = [state_ref[h] for h in hs]
        stb = [sts[h].astype(BF16) for h in hs]
        wq = [jnp.concatenate([uw[h][:, HEAD_DIM:].astype(BF16),
                               (q_ref[bi[h], rows, cols[h]].astype(F32) * eg[h]).astype(BF16)], axis=0) for h in hs]
        ws = [_dot(wq[h], stb[h]) for h in hs]
        vnb = [(uw[h][:, :HEAD_DIM] - ws[h][:CHUNK]).astype(BF16) for h in hs]
        kdt = [(kt[h] * jnp.exp(g_last[h] - gr[h])).astype(BF16) for h in hs]
        o = [ws[h][CHUNK:] + _dot(aqk[h], vnb[h]) for h in hs]
        for h in hs:
            state_ref[h] = sts[h] * jnp.exp(g_last[h]) + _dot(kdt[h], vnb[h])
        for h in hs:
            oh = _rms_norm(o[h], ng) * zs_ref[bi[h], rows, cols[h]].astype(F32)
            o_ref[bi[h], rows, cols[h]] = oh.astype(BF16)
        return carry

    lax.fori_loop(0, ts // CHUNK, chunk, 0)


def _gdn(q, k, v, zs, gbeta, norm_g):
    bsz, seq, _ = q.shape
    ts = min(TS_GDN, seq)
    nb = NB_GDN if bsz % NB_GDN == 0 else 1

    def tok(width):
        return pl.BlockSpec((nb, ts, width), lambda b, s: (b, s, 0))

    return pl.pallas_call(
        _gdn_kernel,
        grid=(bsz // nb, seq // ts),
        in_specs=[tok(GDN_WIDTH)] * 4 + [tok(LANES), pl.BlockSpec((1, HEAD_DIM), lambda b, s: (0, 0))],
        out_specs=tok(GDN_WIDTH),
        out_shape=jax.ShapeDtypeStruct((bsz, seq, GDN_WIDTH), BF16),
        scratch_shapes=[pltpu.VMEM((nb * HEADS, HEAD_DIM, HEAD_DIM), F32)],
        compiler_params=pltpu.CompilerParams(dimension_semantics=("arbitrary", "arbitrary"),
                                             vmem_limit_bytes=VMEM_LIMIT),
        name="gdn",
    )(q, k, v, zs, gbeta, norm_g.reshape(1, HEAD_DIM))


def _mix_kernel(og_ref, sb_ref, ga_ref, gt_ref, x_ref, wog_ref, wos_ref, wmix_ref, gffn_ref, wr_ref, br_ref,
                h1_ref, meta_ref, tw_ref, cnt_ref, xpad_ref,
                carry_ref, xt_ref, dvm_ref, dsm_ref, cvm_ref, csm_ref, zero_ref, sem_rows, sem_s, *, nt, cap, bm):
    tm = x_ref.shape[0]
    i = pl.program_id(0)
    slot = i % 2

    def wait_rows(s):
        for _ in range(TOP_K):
            pltpu.make_async_copy(xt_ref.at[s], xpad_ref.at[pl.ds(0, tm * ROW_TILES), :], sem_rows.at[s]).wait()

    @pl.when(i == 0)
    def _():
        carry_ref[...] = jnp.zeros_like(carry_ref)

    @pl.when(i >= 2)
    def _():
        wait_rows(slot)

    sub = tm // MIX_SPLIT
    rs = [slice(r * sub, (r + 1) * sub) for r in range(MIX_SPLIT)]
    ya = [_dot(og_ref[r, :], wog_ref[...]) for r in rs]
    yb = [_dot(sb_ref[r, :], wos_ref[...]) for r in rs]
    merged = [(ga_ref[r, :].astype(F32) * a + gt_ref[r, :].astype(F32) * b).astype(BF16)
              for r, a, b in zip(rs, ya, yb)]
    h1 = [x_ref[r, :] + _dot(mg, wmix_ref[...]) for r, mg in zip(rs, merged)]
    xn2 = [_rms_norm(h, gffn_ref[...]) for h in h1]
    x_hi = [v.astype(BF16) for v in xn2]
    x_lo = [(v - hi.astype(F32)).astype(BF16) for v, hi in zip(xn2, x_hi)]
    parts = [_dot(hi, wr_ref[...]) + _dot(lo, wr_ref[...]) for hi, lo in zip(x_hi, x_lo)]
    for r, h in zip(rs, h1):
        h1_ref[r, :] = h
    _store_row_tiles(xt_ref.at[slot], jnp.concatenate(xn2, axis=0))
    logits =jnp.concatenate([p[:, :LANES] + p[:, LANES:] for p in parts], axis=0) + br_ref[...]

    lane = lax.broadcasted_iota(I32, (tm, LANES), 1)
    lane_f = lane.astype(F32)
    neg_inf = jnp.float32(-jnp.inf)
    work = jnp.where(lane < N_EXPERTS, logits, neg_inf)
    hits, vals = [], []
    for _ in range(TOP_K):
        m = jnp.max(work, axis=-1, keepdims=True)
        first = jnp.min(jnp.where(work == m, lane_f, float(LANES)), axis=-1, keepdims=True)
        hit = lane_f == first
        work = jnp.where(hit, neg_inf, work)
        hits.append(hit)
        vals.append((m, first))
    exps = [jnp.exp(m - vals[0][0]) for m, _ in vals]
    denom = exps[0] + exps[1] + exps[2] + exps[3]

    sel = jnp.zeros((tm, LANES), F32)
    for hit in hits:
        sel = sel + hit.astype(F32)
    ri = lax.broadcasted_iota(I32, (tm, tm), 0)
    ci = lax.broadcasted_iota(I32, (tm, tm), 1)
    before = (ri > ci).astype(BF16)
    rank_mat = _dot(before, sel.astype(BF16)) + carry_ref[...]
    carry_ref[...] = carry_ref[...] + jnp.sum(sel, axis=0, keepdims=True)
    cnt_ref[...] = carry_ref[...]

    meta = jnp.zeros((tm, LANES), F32)
    rows = jnp.zeros((tm, LANES), F32)
    tw_out = jnp.zeros((tm, LANES), F32)
    for kk in range(TOP_K):
        rk = jnp.sum(jnp.where(hits[kk], rank_mat, 0.0), axis=-1, keepdims=True)
        meta = jnp.where(lane == kk, vals[kk][1], meta)
        meta = jnp.where(lane == TOP_K + kk, rk, meta)
        rows = jnp.where(lane == kk, vals[kk][1] * float(cap) + rk, rows)
        tw_out = jnp.where(lane == kk, exps[kk] / denom, tw_out)
    meta_ref[0] = meta.T[0:2 * TOP_K, :].astype(I32)
    tw_ref[...] = tw_out

    dvm_ref[...] = rows.T[0:SUBLANES, :].astype(I32)
    to_smem = pltpu.make_async_copy(dvm_ref, dsm_ref, sem_s)
    to_smem.start()
    to_smem.wait()

    def issue(g, carry):
        for u in range(ISSUE_UNROLL):
            tok = g * (ISSUE_UNROLL // TOP_K) + u // TOP_K
            pltpu.make_async_copy(_row_tile(xt_ref.at[slot], tok), _row_tile(xpad_ref, dsm_ref[u % TOP_K, tok]),
                                  sem_rows.at[slot]).start()
        return carry

    lax.fori_loop(0, tm * TOP_K // ISSUE_UNROLL, issue, 0)

    @pl.when(i == nt - 1)
    def _():
        wait_rows(slot)
        if nt >= 2:
            wait_rows(1 - slot)
        cvm_ref[...] = jnp.broadcast_to(carry_ref[...], cvm_ref.shape).astype(I32)
        counts_to_smem = pltpu.make_async_copy(cvm_ref, csm_ref, sem_s)
        counts_to_smem.start()
        counts_to_smem.wait()
        zero_ref[...] = jnp.zeros_like(zero_ref)
        for e in range(N_EXPERTS):
            count = csm_ref[0, e]
            padded = (count + (bm - 1)) // bm * bm

            def zero_row(r, carry):
                pltpu.make_async_copy(zero_ref, _row_tile(xpad_ref, e * cap + r), sem_s).start()
                return carry

            def zero_wait(r, carry):
                pltpu.make_async_copy(zero_ref, _row_tile(xpad_ref, 0), sem_s).wait()
                return carry

            lax.fori_loop(count, padded, zero_row, 0)
            lax.fori_loop(count, padded, zero_wait, 0)


def _mix(og, sb, ga, gt, x, w_o_gdn, w_o_sconv, w_mix_out, norm_ffn_g, w_router, b_router, cap, bm):
    t = x.shape[0]
    tm = min(TM_MIX, t)
    nt = t // tm
    assert N_EXPERTS * cap < 2 ** 24, "row indices are formed in f32"
    wr = jnp.zeros((D_MODEL, LANES), F32).at[:, :N_EXPERTS].set(w_router)
    wr_hi = wr.astype(BF16)
    wr = jnp.concatenate([wr_hi, (wr - wr_hi.astype(F32)).astype(BF16)], axis=1)
    br = jnp.zeros((1, LANES), F32).at[0, :N_EXPERTS].set(b_router)

    def const(shape):
        return pl.BlockSpec(shape, lambda i: (0,) * len(shape), pipeline_mode=pl.Buffered(1))

    def tok(rows, width):
        return pl.BlockSpec((rows, width), lambda i: (i, 0))

    return pl.pallas_call(
        functools.partial(_mix_kernel, nt=nt, cap=cap, bm=bm),
        grid=(nt,),
        in_specs=[tok(tm, D_MODEL)] * 5 + [const((D_MODEL, D_MODEL))] * 3
        + [const((1, D_MODEL)), const((D_MODEL, 2 * LANES)), const((1, LANES))],
        out_specs=[tok(tm, D_MODEL), pl.BlockSpec((1, 2 * TOP_K, tm), lambda i: (i, 0, 0)), tok(tm, LANES),
                   pl.BlockSpec((1, LANES), lambda i: (0, 0)), pl.BlockSpec(memory_space=pl.ANY)],
        out_shape=[jax.ShapeDtypeStruct((t, D_MODEL), F32), jax.ShapeDtypeStruct((nt, 2 * TOP_K, tm), I32),
                   jax.ShapeDtypeStruct((t, LANES), F32), jax.ShapeDtypeStruct((1, LANES), F32),
                   jax.ShapeDtypeStruct((N_EXPERTS * cap * ROW_TILES, LANES), F32)],
        scratch_shapes=[pltpu.VMEM((1, LANES), F32), pltpu.VMEM((2, tm * ROW_TILES, LANES), F32),
                        pltpu.VMEM((SUBLANES, tm), I32), pltpu.SMEM((SUBLANES, tm), I32),
                        pltpu.VMEM((SUBLANES, LANES), I32), pltpu.SMEM((SUBLANES, LANES), I32),
                        pltpu.VMEM((ROW_TILES, LANES), F32),
                        pltpu.SemaphoreType.DMA((2,)), pltpu.SemaphoreType.DMA(())],
        compiler_params=pltpu.CompilerParams(dimension_semantics=("arbitrary",), vmem_limit_bytes=VMEM_LIMIT,
                                             has_side_effects=True, disable_bounds_checks=True),
        name="mix_router",
    )(og, sb, ga, gt, x, w_o_gdn.astype(BF16), w_o_sconv.astype(BF16), w_mix_out.astype(BF16),
      norm_ffn_g.reshape(1, D_MODEL), wr, br)


def _moe_kernel(be_ref, nu_ref, xr_ref, x_ref, wgu_ref, bgu_ref, wd_ref, bd_ref, y_ref, wgu_bf, wd_bf):
    bm = x_ref.shape[0] // ROW_TILES
    i = pl.program_id(0)

    @pl.when(i < nu_ref[0])
    def _():
        @pl.when((i == 0) | (be_ref[i] != be_ref[jnp.maximum(i - 1, 0)]))
        def _():
            def cast_rows(r, carry):
                rows = pl.ds(pl.multiple_of(r * CAST_ROWS, CAST_ROWS), CAST_ROWS)
                wgu_bf[rows, :] = wgu_ref[rows, :].astype(BF16)
                wd_bf[rows, :] = wd_ref[rows, :].astype(BF16)
                return carry

            lax.fori_loop(0, D_MODEL // CAST_ROWS, cast_rows, 0)

        xb = jnp.concatenate([_load_row_tiles(x_ref, bm, s).astype(BF16) for s in range(ROW_TILES)], axis=1)
        gu = _dot(xb, wgu_bf[...]) + bgu_ref[...]
        gate = jnp.minimum(gu[:, :D_FF], SWIGLU_LIMIT)
        up = jnp.clip(gu[:, D_FF:], -SWIGLU_LIMIT, SWIGLU_LIMIT)
        hid = (up + 1.0) * gate * _sigmoid(SWIGLU_ALPHA * gate)
        _store_row_tiles(y_ref, _dot(hid.astype(BF16), wd_bf[...]) + bd_ref[...])


def _moe(xpad, block_e, n_used, x_block, n_slots, w_gate_up, b_gate_up, w_down, b_down, bm):
    nb = n_slots // bm

    def rows(i, be, nu, xr):
        return (jnp.minimum(i, nu[0] - 1), 0)

    def x_rows(i, be, nu, xr):
        return (xr[i], 0)

    def expert(i, be, nu, xr):
        return (be[i], 0, 0)

    grid_spec = pltpu.PrefetchScalarGridSpec(
        num_scalar_prefetch=3,
        grid=(nb,),
        in_specs=[pl.BlockSpec((bm * ROW_TILES, LANES), x_rows),
                  pl.BlockSpec((None, D_MODEL, 2 * D_FF), expert),
                  pl.BlockSpec((None, 1, 2 * D_FF), expert),
                  pl.BlockSpec((None, D_FF, D_MODEL), expert),
                  pl.BlockSpec((None, 1, D_MODEL), expert)],
        out_specs=pl.BlockSpec((bm * ROW_TILES, LANES), rows),
        scratch_shapes=[pltpu.VMEM((D_MODEL, 2 * D_FF), BF16), pltpu.VMEM((D_FF, D_MODEL), BF16)],
    )
    return pl.pallas_call(
        _moe_kernel,
        grid_spec=grid_spec,
        out_shape=jax.ShapeDtypeStruct((n_slots * ROW_TILES, LANES), F32),
        compiler_params=pltpu.CompilerParams(dimension_semantics=("arbitrary",), vmem_limit_bytes=VMEM_LIMIT),
        name="moe_mlp",
    )(block_e, n_used, x_block, xpad, w_gate_up, b_gate_up.reshape(N_EXPERTS, 1, 2 * D_FF),
      w_down, b_down.reshape(N_EXPERTS, 1, D_MODEL))


def _combine_kernel(dest_ref, dest_next_ref, h1_ref, tw_ref, gf_ref, ypad_ref, out_ref, buf_ref, sem):
    tm = h1_ref.shape[0]
    i = pl.program_id(0)
    cur = i % 2

    def gather(d_ref, parity):
        def issue(g, carry):
            for u in range(ISSUE_UNROLL):
                tok = g * (ISSUE_UNROLL // TOP_K) + u // TOP_K
                pltpu.make_async_copy(_row_tile(ypad_ref, d_ref[0, 0, g * ISSUE_UNROLL + u]),
                                      _row_tile(buf_ref.at[parity * TOP_K + u % TOP_K], tok),
                                      sem.at[parity]).start()
            return carry

        lax.fori_loop(0, tm * TOP_K // ISSUE_UNROLL, issue, 0)

    @pl.when(i == 0)
    def _():
        gather(dest_ref, 0)

    @pl.when(i + 1 < pl.num_programs(0))
    def _():
        gather(dest_next_ref, 1 - cur)

    for kk in range(TOP_K):
        pltpu.make_async_copy(ypad_ref.at[pl.ds(0, tm * ROW_TILES), :], buf_ref.at[cur * TOP_K + kk],
                              sem.at[cur]).wait()

    tw = tw_ref[...]
    accs = []
    ssq = jnp.zeros((tm, 1), F32)
    for s in range(ROW_TILES):
        acc = h1_ref[:, s * LANES:(s + 1) * LANES]
        for kk in range(TOP_K):
            acc = acc + tw[:, kk:kk + 1] * _load_row_tiles(buf_ref.at[cur * TOP_K + kk], tm, s)
        ssq = ssq + jnp.sum(acc * acc, axis=-1, keepdims=True)
        accs.append(acc)
    inv = lax.rsqrt(ssq * (1.0 / D_MODEL) + NORM_EPS)
    for s in range(ROW_TILES):
        out_ref[:, s * LANES:(s + 1) * LANES] = accs[s] * inv * gf_ref[:, s * LANES:(s + 1) * LANES]


def _combine(ypad, dest, h1, tw, norm_final_g):
    t = h1.shape[0]
    tm = min(TM_COMB, t)
    nt = t // tm
    dest3 = dest.reshape(nt, 1, tm * TOP_K)
    return pl.pallas_call(
        _combine_kernel,
        grid=(nt,),
        in_specs=[pl.BlockSpec((1, 1, tm * TOP_K), lambda i: (i, 0, 0), memory_space=pltpu.SMEM),
                  pl.BlockSpec((1, 1, tm * TOP_K), lambda i: (jnp.minimum(i + 1, nt - 1), 0, 0),
                               memory_space=pltpu.SMEM),
                  pl.BlockSpec((tm, D_MODEL), lambda i: (i, 0)),
                  pl.BlockSpec((tm, LANES), lambda i: (i, 0)),
                  pl.BlockSpec((1, D_MODEL), lambda i: (0, 0)),
                  pl.BlockSpec(memory_space=pl.ANY)],
        out_specs=pl.BlockSpec((tm, D_MODEL), lambda i: (i, 0)),
        out_shape=jax.ShapeDtypeStruct((t, D_MODEL), F32),
        scratch_shapes=[pltpu.VMEM((2 * TOP_K, tm * ROW_TILES, LANES), F32), pltpu.SemaphoreType.DMA((2,))],
        compiler_params=pltpu.CompilerParams(dimension_semantics=("arbitrary",), vmem_limit_bytes=VMEM_LIMIT,
                                             disable_bounds_checks=True),
        name="combine",
    )(dest3, dest3, h1, tw, norm_final_g.reshape(1, D_MODEL), ypad)


def _routing_tables(meta, cnt, t, bm, cap):
    counts = cnt[0, :N_EXPERTS].astype(I32)
    pcounts = ((counts + bm - 1) // bm) * bm
    pend = jnp.cumsum(pcounts)
    pstart = (pend - pcounts).astype(I32)
    idx, rank = meta[:, :TOP_K, :], meta[:, TOP_K:, :]
    onehot = idx[..., None] == jnp.arange(N_EXPERTS, dtype=I32)
    dest = jnp.sum(jnp.where(onehot, pstart, 0), axis=-1).astype(I32) + rank
    dest = dest.transpose(0, 2, 1).reshape(-1)
    n_slots = ((t * TOP_K + N_EXPERTS * (bm - 1) + bm - 1) // bm) * bm
    nb = n_slots // bm
    n_used = pend[-1] // bm
    starts = jnp.arange(nb, dtype=I32) * bm
    block_e = jnp.minimum(jnp.sum((pend[None, :] <= starts[:, None]).astype(I32), axis=1), N_EXPERTS - 1)
    x_block = block_e * (cap // bm) + (starts - pstart[block_e]) // bm
    last = jnp.maximum(n_used - 1, 0)
    used = jnp.arange(nb) < n_used
    block_e = jnp.where(used, block_e, block_e[last]).astype(I32)
    x_block = jnp.where(used, x_block, x_block[last]).astype(I32)
    return dest, block_e, n_used.reshape(1).astype(I32), x_block, n_slots


def kernel(x, norm_mix_g, w_in, gdn_conv_w, gdn_A_log, gdn_dt_bias, gdn_norm_g, w_o_gdn, sconv_w, w_o_sconv,
           w_mix_out, norm_ffn_g, w_router, b_router, w_gate_up, b_gate_up, w_down, b_down, norm_final_g):
    bsz, seq, _ = x.shape
    t = bsz * seq
    assert norm_mix_g.shape[0] == 1, "single-layer block only"
    q, k, v, zs, gbeta, sb, ga, gt = _inproj(x, norm_mix_g[0], w_in[0], gdn_conv_w[0], gdn_A_log[0],
                                             gdn_dt_bias[0], sconv_w[0])
    og = _gdn(q, k, v, zs, gbeta, gdn_norm_g[0])

    def flat(a):
        return a.reshape(t, a.shape[-1])

    bm = BM_MOE
    cap = pl.cdiv(t, bm) * bm
    h1, meta, tw, cnt, xpad = _mix(flat(og), flat(sb), flat(ga), flat(gt), flat(x), w_o_gdn[0], w_o_sconv[0],
                                   w_mix_out[0], norm_ffn_g[0], w_router[0], b_router[0], cap, bm)
    dest, block_e, n_used, x_block, n_slots = _routing_tables(meta, cnt, t, bm, cap)
    ypad = _moe(xpad, block_e, n_used, x_block, n_slots, w_gate_up[0], b_gate_up[0], w_down[0], b_down[0], bm)
    return _combine(ypad, dest, h1, tw, norm_final_g).reshape(bsz, seq, D_MODEL)
```

```python
import functools

import jax
import jax.numpy as jnp
from jax import lax
from jax.experimental import pallas as pl
from jax.experimental.pallas import tpu as pltpu

F32 = jnp.float32
BF16 = jnp.bfloat16
I32 = jnp.int32

D_MODEL = 1024
HEADS = 8
HEAD_DIM = 128
GDN_WIDTH = HEADS * HEAD_DIM
GDN_CONV = 4
CHUNK = 128
INV_BLOCK = 32
SC_CONV = 3
N_EXPERTS = 32
TOP_K = 4
D_FF = 1024
SWIGLU_LIMIT = 7.0
SWIGLU_ALPHA = 1.702
NORM_EPS = 1e-6

LANES = 128
SUBLANES = 8
VMEM_LIMIT = 56 * 1024 * 1024

TM_IN = 512
CT_IN = 256
TS_GDN = 512
NB_GDN = 2
TM_MIX = 512
ROW_BUFS = 3
MIX_SPLIT = 4
BM_MOE = 512
TM_COMB = 256
ISSUE_UNROLL = 8
CAST_ROWS = 128


NEG_LOG2E = -1.4426950408889634


def _sigmoid(x):
    return 1.0 / (1.0 + jnp.exp2(x * NEG_LOG2E))


def _silu(x):
    return x * _sigmoid(x)


def _softplus(x):
    return jnp.maximum(x, 0.0) + jnp.log(1.0 + jnp.exp(-jnp.abs(x)))


def _rms_norm(x, g):
    return x * lax.rsqrt(jnp.mean(x * x, axis=-1, keepdims=True) + NORM_EPS) * g


def _dot(a, b):
    return jnp.dot(a, b, preferred_element_type=F32)


def _dot_nt(a, b):
    return lax.dot_general(a, b, (((1,), (1,)), ((), ())), preferred_element_type=F32)


def _dot_tn(a, b):
    return lax.dot_general(a, b, (((0,), (0,)), ((), ())), preferred_element_type=F32)


ROW_TILES = D_MODEL // LANES


def _store_row_tiles(ref, val):
    m = val.shape[0]
    for s in range(ROW_TILES):
        ref[pl.ds(s, m, stride=ROW_TILES), :] = val[:, s * LANES:(s + 1) * LANES]


def _load_row_tiles(ref, m, s):
    return ref[pl.ds(s, m, stride=ROW_TILES), :]


def _row_tile(ref, row):
    return ref.at[pl.ds(pl.multiple_of(row * ROW_TILES, ROW_TILES), ROW_TILES), :]


def _causal_conv(p, prev, w, stage_ref):
    taps = w.shape[0]
    tm = p.shape[0]
    stage_ref[0:SUBLANES, :] = prev
    stage_ref[SUBLANES:SUBLANES + tm, :] = p
    y = w[taps - 1:taps] * p
    for s in range(1, taps):
        y = y + w[taps - 1 - s:taps - s] * stage_ref[SUBLANES - s:SUBLANES - s + tm, :]
    return y


def _inproj_kernel(x_ref, ng_ref, wqkv_ref, wz_ref, wab_ref, wxs_ref, wgb_ref, wgc_ref, wga_ref, wgt_ref,
                   cw_ref, scw_ref, alog_ref, dtb_ref,
                   q_ref, k_ref, v_ref, zs_ref, gbeta_ref, sb_ref, ga_ref, gt_ref,
                   carry_qkv, carry_u, xn_scr, stage_ref):
    tm = x_ref.shape[0]

    @pl.when(pl.program_id(1) == 0)
    def _():
        carry_qkv[...] = jnp.zeros_like(carry_qkv)
        carry_u[...] = jnp.zeros_like(carry_u)

    x = x_ref[...]
    xn_scr[...] = _rms_norm(x, ng_ref[...]).astype(BF16)

    q_scale = HEAD_DIM ** -0.5
    outs = (q_ref, k_ref, v_ref)
    for ci in range(3 * GDN_WIDTH // CT_IN):
        c0 = ci * CT_IN
        p = _dot(xn_scr[...], wqkv_ref[:, c0:c0 + CT_IN])
        y = _causal_conv(p, carry_qkv[:, c0:c0 + CT_IN], cw_ref[:, c0:c0 + CT_IN], stage_ref.at[ci % 2])
        carry_qkv[:, c0:c0 + CT_IN] = p[tm - SUBLANES:tm]
        y = _silu(y)
        which = c0 // GDN_WIDTH
        o0 = c0 - which * GDN_WIDTH
        for hh in range(CT_IN // HEAD_DIM):
            yh = y[:, hh * HEAD_DIM:(hh + 1) * HEAD_DIM]
            if which < 2:
                yh = yh * lax.rsqrt(jnp.sum(yh * yh, axis=-1, keepdims=True) + NORM_EPS)
            if which == 0:
                yh = yh * q_scale
            outs[which][:, o0 + hh * HEAD_DIM:o0 + (hh + 1) * HEAD_DIM] = yh.astype(BF16)

    for ci in range(D_MODEL // CT_IN):
        c0 = ci * CT_IN
        zs_ref[:, c0:c0 + CT_IN] = _silu(_dot(xn_scr[...], wz_ref[:, c0:c0 + CT_IN])).astype(BF16)
        ga_ref[:, c0:c0 + CT_IN] = _sigmoid(_dot(xn_scr[...], wga_ref[:, c0:c0 + CT_IN])).astype(BF16)
        gt_ref[:, c0:c0 + CT_IN] = _sigmoid(_dot(xn_scr[...], wgt_ref[:, c0:c0 + CT_IN])).astype(BF16)

    for ci in range(D_MODEL // CT_IN):
        c0 = ci * CT_IN
        xs = _dot(xn_scr[...], wxs_ref[:, c0:c0 + CT_IN])
        gc = _dot(xn_scr[...], wgc_ref[:, c0:c0 + CT_IN])
        u = gc * xs
        y = _causal_conv(u, carry_u[:, c0:c0 + CT_IN], scw_ref[:, c0:c0 + CT_IN], stage_ref.at[ci % 2])
        carry_u[:, c0:c0 + CT_IN] = u[tm - SUBLANES:tm]
        gb = _dot(xn_scr[...], wgb_ref[:, c0:c0 + CT_IN])
        sb_ref[:, c0:c0 + CT_IN] = (gb * y).astype(BF16)

    ab = _dot(xn_scr[...], wab_ref[...])
    lane = lax.broadcasted_iota(I32, ab.shape, 1)
    g = -jnp.exp(alog_ref[...]) * _softplus(ab + dtb_ref[...])
    g_hi = g.astype(BF16).astype(F32)
    gbeta_ref[...] = jnp.where(lane < HEADS, g_hi, jnp.where(lane < 2 * HEADS, _sigmoid(ab), g - g_hi))


def _inproj(x, norm_g, w_in, conv_w, a_log, dt_bias, sconv_w):
    bsz, seq, _ = x.shape
    tm = min(TM_IN, seq)
    o = 0
    w = {}
    for name, width in (("qkv", 3 * GDN_WIDTH), ("z", GDN_WIDTH), ("a", HEADS), ("b", HEADS),
                        ("xs", D_MODEL), ("gb", D_MODEL), ("gc", D_MODEL), ("ga", D_MODEL), ("gt", D_MODEL)):
        w[name] = w_in[:, o:o + width]
        o += width
    wab = (jnp.zeros((D_MODEL, LANES), F32).at[:, :HEADS].set(w["a"]).at[:, HEADS:2 * HEADS].set(w["b"])
           .at[:, 2 * HEADS:3 * HEADS].set(w["a"]))
    alog = jnp.zeros((1, LANES), F32).at[0, :HEADS].set(a_log).at[0, 2 * HEADS:3 * HEADS].set(a_log)
    dtb = jnp.zeros((1, LANES), F32).at[0, :HEADS].set(dt_bias).at[0, 2 * HEADS:3 * HEADS].set(dt_bias)
    weights = [w["qkv"], w["z"], wab, w["xs"], w["gb"], w["gc"], w["ga"], w["gt"]]
    weights = [a.astype(BF16) for a in weights]

    def const(shape):
        return pl.BlockSpec(shape, lambda b, s: (0,) * len(shape), pipeline_mode=pl.Buffered(1))

    def tok(width):
        return pl.BlockSpec((None, tm, width), lambda b, s: (b, s, 0))

    out_bf = jax.ShapeDtypeStruct((bsz, seq, D_MODEL), BF16)
    return pl.pallas_call(
        _inproj_kernel,
        grid=(bsz, seq // tm),
        in_specs=[tok(D_MODEL), const((1, D_MODEL))] + [const(a.shape) for a in weights]
        + [const((GDN_CONV, 3 * GDN_WIDTH)), const((SC_CONV, D_MODEL)), const((1, LANES)), const((1, LANES))],
        out_specs=[tok(D_MODEL)] * 4 + [tok(LANES)] + [tok(D_MODEL)] * 3,
        out_shape=[out_bf] * 4 + [jax.ShapeDtypeStruct((bsz, seq, LANES), F32)] + [out_bf] * 3,
        scratch_shapes=[pltpu.VMEM((SUBLANES, 3 * GDN_WIDTH), F32), pltpu.VMEM((SUBLANES, D_MODEL), F32),
                        pltpu.VMEM((tm, D_MODEL), BF16), pltpu.VMEM((2, SUBLANES + tm, CT_IN), F32)],
        compiler_params=pltpu.CompilerParams(dimension_semantics=("arbitrary", "arbitrary"),
                                             vmem_limit_bytes=VMEM_LIMIT),
        name="inproj",
    )(x, norm_g.reshape(1, D_MODEL), *weights, conv_w, sconv_w, alog, dtb)


def _gdn_kernel(q_ref, k_ref, v_ref, zs_ref, gbeta_ref, ng_ref, o_ref, state_ref):
    nb, ts = q_ref.shape[0], q_ref.shape[1]

    @pl.when(pl.program_id(1) == 0)
    def _():
        state_ref[...] = jnp.zeros_like(state_ref)

    ri = lax.broadcasted_iota(I32, (CHUNK, CHUNK), 0)
    ci = lax.broadcasted_iota(I32, (CHUNK, CHUNK), 1)
    causal = ri >= ci
    strict = ri > ci
    ltri = causal.astype(BF16)
    eye = (ri == ci).astype(F32)
    sh = INV_BLOCK.bit_length() - 1
    diag_blk = (ri >> sh) == (ci >> sh)
    off_blks = []
    while (1 << sh) < CHUNK:
        off_blks.append(((ri >> (sh + 1)) == (ci >> (sh + 1))) & ((ri >> sh) != (ci >> sh)) & strict)
        sh += 1
    ng = ng_ref[...]
    hs = range(nb * HEADS)
    bi = [u // HEADS for u in hs]
    hi = [u % HEADS for u in hs]

    def chunk(c, carry):
        rows = pl.ds(pl.multiple_of(c * CHUNK, CHUNK), CHUNK)
        cols = [slice(hi[h] * HEAD_DIM, (hi[h] + 1) * HEAD_DIM) for h in hs]
        gbc = [gbeta_ref[b, rows, :] for b in range(nb)]
        gcum = [_dot(ltri, gbc[b].astype(BF16)) for b in range(nb)]
        gcum_t = [gcum[b].T for b in range(nb)]
        gc = [gcum[bi[h]][:, hi[h]:hi[h] + 1] + gcum[bi[h]][:, 2 * HEADS + hi[h]:2 * HEADS + hi[h] + 1]
              for h in hs]
        gr = [gcum_t[bi[h]][hi[h]:hi[h] + 1, :] + gcum_t[bi[h]][2 * HEADS + hi[h]:2 * HEADS + hi[h] + 1, :]
              for h in hs]
        beta = [gbc[bi[h]][:, HEADS + hi[h]:HEADS + hi[h] + 1] for h in hs]
        g_last = [gc[h][CHUNK - 1:CHUNK, :] for h in hs]
        decay = [jnp.where(causal, jnp.exp(jnp.where(causal, gc[h] - gr[h], 0.0)), 0.0) for h in hs]
        eg = [jnp.exp(gc[h]) for h in hs]

        kf = [k_ref[bi[h], rows, cols[h]].astype(F32) for h in hs]
        kb = [kf[h] * beta[h] for h in hs]
        kt = [kf[h].T for h in hs]
        ktb = [kt[h].astype(BF16) for h in hs]
        lhs = [jnp.concatenate([kb[h].astype(BF16), q_ref[bi[h], rows, cols[h]]], axis=0) for h in hs]
        aq = [_dot(lhs[h], ktb[h]) for h in hs]
        aqk = [(aq[h][CHUNK:] * decay[h]).astype(BF16) for h in hs]
        lm = [jnp.where(strict, aq[h][:CHUNK] * decay[h], 0.0) for h in hs]
        m = [jnp.where(diag_blk, -lm[h], 0.0) for h in hs]
        tinv = [eye + m[h] for h in hs]
        mb = [m[h].astype(BF16) for h in hs]
        m = [_dot(mb[h], mb[h]) for h in hs]
        levels = INV_BLOCK.bit_length() - 1
        for j in range(1, levels):
            mb = [m[h].astype(BF16) for h in hs]
            if j < levels - 1:
                st = [_dot(jnp.concatenate([tinv[h].astype(BF16), mb[h]], axis=0), mb[h]) for h in hs]
                tinv = [tinv[h] + st[h][:CHUNK] for h in hs]
                m = [st[h][CHUNK:] for h in hs]
            else:
                tinv = [tinv[h] + _dot(tinv[h].astype(BF16), mb[h]) for h in hs]
        for off in off_blks:
            tb = [tinv[h].astype(BF16) for h in hs]
            ct = [_dot(jnp.where(off, lm[h], 0.0).astype(BF16), tb[h]) for h in hs]
            tinv = [tinv[h] - _dot(tb[h], ct[h].astype(BF16)) for h in hs]
        rhs = [jnp.concatenate([(v_ref[bi[h], rows, cols[h]].astype(F32) * beta[h]).astype(BF16),
                                (kb[h] * eg[h]).astype(BF16)], axis=1) for h in hs]
        uw = [_dot(tinv[h].astype(BF16), rhs[h]) for h in hs]

        sts = [state_ref[h] for h in hs]
        stb = [sts[h].astype(BF16) for h in hs]
        wq = [jnp.concatenate([uw[h][:, HEAD_DIM:].astype(BF16),
                               (q_ref[bi[h], rows, cols[h]].astype(F32) * eg[h]).astype(BF16)], axis=0) for h in hs]
        ws = [_dot(wq[h], stb[h]) for h in hs]
        vnb = [(uw[h][:, :HEAD_DIM] - ws[h][:CHUNK]).astype(BF16) for h in hs]
        kdt = [(kt[h] * jnp.exp(g_last[h] - gr[h])).astype(BF16) for h in hs]
        o = [ws[h][CHUNK:] + _dot(aqk[h], vnb[h]) for h in hs]
        for h in hs:
            state_ref[h] = sts[h] * jnp.exp(g_last[h]) + _dot(kdt[h], vnb[h])
        for h in hs:
            oh = _rms_norm(o[h], ng) * zs_ref[bi[h], rows, cols[h]].astype(F32)
            o_ref[bi[h], rows, cols[h]] = oh.astype(BF16)
        return carry

    lax.fori_loop(0, ts // CHUNK, chunk, 0)


def _gdn(q, k, v, zs, gbeta, norm_g):
    bsz, seq, _ = q.shape
    ts = min(TS_GDN, seq)
    nb = NB_GDN if bsz % NB_GDN == 0 else 1

    def tok(width):
        return pl.BlockSpec((nb, ts, width), lambda b, s: (b, s, 0))

    return pl.pallas_call(
        _gdn_kernel,
        grid=(bsz // nb, seq // ts),
        in_specs=[tok(GDN_WIDTH)] * 4 + [tok(LANES), pl.BlockSpec((1, HEAD_DIM), lambda b, s: (0, 0))],
        out_specs=tok(GDN_WIDTH),
        out_shape=jax.ShapeDtypeStruct((bsz, seq, GDN_WIDTH), BF16),
        scratch_shapes=[pltpu.VMEM((nb * HEADS, HEAD_DIM, HEAD_DIM), F32)],
        compiler_params=pltpu.CompilerParams(dimension_semantics=("arbitrary", "arbitrary"),
                                             vmem_limit_bytes=VMEM_LIMIT),
        name="gdn",
    )(q, k, v, zs, gbeta, norm_g.reshape(1, HEAD_DIM))


def _mix_kernel(og_ref, sb_ref, ga_ref, gt_ref, x_ref, wog_ref, wos_ref, wmix_ref, gffn_ref, wr_ref, br_ref,
                h1_ref, meta_ref, tw_ref, cnt_ref, xpad_ref,
                carry_ref, xt_ref, dvm_ref, dsm_ref, cvm_ref, csm_ref, zero_ref, sem_rows, sem_s, sem_z,
                *, nt, cap, bm):
    tm = x_ref.shape[0]
    i = pl.program_id(0)
    slot = i % ROW_BUFS

    def wait_rows(s):
        for _ in range(TOP_K):
            pltpu.make_async_copy(xt_ref.at[s], xpad_ref.at[pl.ds(0, tm * ROW_TILES), :], sem_rows.at[s]).wait()

    def slots_copy(par):
        return pltpu.make_async_copy(dvm_ref.at[par], dsm_ref.at[par], sem_s.at[par])

    def issue_rows(s, par):
        def issue(g, carry):
            for u in range(ISSUE_UNROLL):
                tok = g * (ISSUE_UNROLL // TOP_K) + u // TOP_K
                pltpu.make_async_copy(_row_tile(xt_ref.at[s], tok),
                                      _row_tile(xpad_ref, dsm_ref[par, u % TOP_K, tok]), sem_rows.at[s]).start()
            return carry

        lax.fori_loop(0, tm * TOP_K // ISSUE_UNROLL, issue, 0)

    @pl.when(i == 0)
    def _():
        carry_ref[...] = jnp.zeros_like(carry_ref)

    @pl.when(i >= ROW_BUFS)
    def _():
        wait_rows(slot)

    sub = tm // MIX_SPLIT
    rs = [slice(r * sub, (r + 1) * sub) for r in range(MIX_SPLIT)]
    ya = [_dot(og_ref[r, :], wog_ref[...]) for r in rs]
    yb = [_dot(sb_ref[r, :], wos_ref[...]) for r in rs]
    merged = [(ga_ref[r, :].astype(F32) * a + gt_ref[r, :].astype(F32) * b).astype(BF16)
              for r, a, b in zip(rs, ya, yb)]
    h1 = [x_ref[r, :] + _dot(mg, wmix_ref[...]) for r, mg in zip(rs, merged)]
    xn2 = [_rms_norm(h, gffn_ref[...]) for h in h1]
    x_hi = [v.astype(BF16) for v in xn2]
    x_lo = [(v - hi.astype(F32)).astype(BF16) for v, hi in zip(xn2, x_hi)]
    parts = [_dot(hi, wr_ref[...]) + _dot(lo, wr_ref[...]) for hi, lo in zip(x_hi, x_lo)]
    for r, h in zip(rs, h1):
        h1_ref[r, :] = h
    _store_row_tiles(xt_ref.at[slot], jnp.concatenate(xn2, axis=0))
    logits =jnp.concatenate([p[:, :LANES] + p[:, LANES:] for p in parts], axis=0) + br_ref[...]

    lane = lax.broadcasted_iota(I32, (tm, LANES), 1)
    lane_f = lane.astype(F32)
    neg_inf = jnp.float32(-jnp.inf)
    work = jnp.where(lane < N_EXPERTS, logits, neg_inf)
    hits, vals = [], []
    for _ in range(TOP_K):
        m = jnp.max(work, axis=-1, keepdims=True)
        first = jnp.min(jnp.where(work == m, lane_f, float(LANES)), axis=-1, keepdims=True)
        hit = lane_f == first
        work = jnp.where(hit, neg_inf, work)
        hits.append(hit)
        vals.append((m, first))
    exps = [jnp.exp(m - vals[0][0]) for m, _ in vals]
    denom = exps[0] + exps[1] + exps[2] + exps[3]

    sel = jnp.zeros((tm, LANES), F32)
    for hit in hits:
        sel = sel + hit.astype(F32)
    ri = lax.broadcasted_iota(I32, (tm, tm), 0)
    ci = lax.broadcasted_iota(I32, (tm, tm), 1)
    before = (ri > ci).astype(BF16)
    rank_mat = _dot(before, sel.astype(BF16)) + carry_ref[...]
    carry_ref[...] = carry_ref[...] + jnp.sum(sel, axis=0, keepdims=True)
    cnt_ref[...] = carry_ref[...]

    meta = jnp.zeros((tm, LANES), F32)
    rows = jnp.zeros((tm, LANES), F32)
    tw_out = jnp.zeros((tm, LANES), F32)
    for kk in range(TOP_K):
        rk = jnp.sum(jnp.where(hits[kk], rank_mat, 0.0), axis=-1, keepdims=True)
        meta = jnp.where(lane == kk, vals[kk][1], meta)
        meta = jnp.where(lane == TOP_K + kk, rk, meta)
        rows = jnp.where(lane == kk, vals[kk][1] * float(cap) + rk, rows)
        tw_out = jnp.where(lane == kk, exps[kk] / denom, tw_out)
    meta_ref[0] = meta.T[0:2 * TOP_K, :].astype(I32)
    tw_ref[...] = tw_out

    par = i % 2
    dvm_ref[par] = rows.T[0:SUBLANES, :].astype(I32)
    slots_copy(par).start()

    @pl.when(i >= 1)
    def _():
        slots_copy(1 - par).wait()
        issue_rows((i + ROW_BUFS - 1) % ROW_BUFS, 1 - par)

    @pl.when(i == nt - 1)
    def _():
        slots_copy(par).wait()
        issue_rows(slot, par)
        for back in range(min(ROW_BUFS, nt)):
            wait_rows((i + ROW_BUFS - back) % ROW_BUFS)
        cvm_ref[...] = jnp.broadcast_to(carry_ref[...], cvm_ref.shape).astype(I32)
        counts_to_smem = pltpu.make_async_copy(cvm_ref, csm_ref, sem_z)
        counts_to_smem.start()
        counts_to_smem.wait()
        zero_ref[...] = jnp.zeros_like(zero_ref)
        for e in range(N_EXPERTS):
            count = csm_ref[0, e]
            padded = (count + (bm - 1)) // bm * bm

            def zero_row(r, carry):
                pltpu.make_async_copy(zero_ref, _row_tile(xpad_ref, e * cap + r), sem_z).start()
                return carry

            def zero_wait(r, carry):
                pltpu.make_async_copy(zero_ref, _row_tile(xpad_ref, 0), sem_z).wait()
                return carry

            lax.fori_loop(count, padded, zero_row, 0)
            lax.fori_loop(count, padded, zero_wait, 0)


def _mix(og, sb, ga, gt, x, w_o_gdn, w_o_sconv, w_mix_out, norm_ffn_g, w_router, b_router, cap, bm):
    t = x.shape[0]
    tm = min(TM_MIX, t)
    nt = t // tm
    assert N_EXPERTS * cap < 2 ** 24, "row indices are formed in f32"
    wr = jnp.zeros((D_MODEL, LANES), F32).at[:, :N_EXPERTS].set(w_router)
    wr_hi = wr.astype(BF16)
    wr = jnp.concatenate([wr_hi, (wr - wr_hi.astype(F32)).astype(BF16)], axis=1)
    br = jnp.zeros((1, LANES), F32).at[0, :N_EXPERTS].set(b_router)

    def const(shape):
        return pl.BlockSpec(shape, lambda i: (0,) * len(shape), pipeline_mode=pl.Buffered(1))

    def tok(rows, width):
        return pl.BlockSpec((rows, width), lambda i: (i, 0))

    return pl.pallas_call(
        functools.partial(_mix_kernel, nt=nt, cap=cap, bm=bm),
        grid=(nt,),
        in_specs=[tok(tm, D_MODEL)] * 5 + [const((D_MODEL, D_MODEL))] * 3
        + [const((1, D_MODEL)), const((D_MODEL, 2 * LANES)), const((1, LANES))],
        out_specs=[tok(tm, D_MODEL), pl.BlockSpec((1, 2 * TOP_K, tm), lambda i: (i, 0, 0)), tok(tm, LANES),
                   pl.BlockSpec((1, LANES), lambda i: (0, 0)), pl.BlockSpec(memory_space=pl.ANY)],
        out_shape=[jax.ShapeDtypeStruct((t, D_MODEL), F32), jax.ShapeDtypeStruct((nt, 2 * TOP_K, tm), I32),
                   jax.ShapeDtypeStruct((t, LANES), F32), jax.ShapeDtypeStruct((1, LANES), F32),
                   jax.ShapeDtypeStruct((N_EXPERTS * cap * ROW_TILES, LANES), F32)],
        scratch_shapes=[pltpu.VMEM((1, LANES), F32), pltpu.VMEM((ROW_BUFS, tm * ROW_TILES, LANES), F32),
                        pltpu.VMEM((2, SUBLANES, tm), I32), pltpu.SMEM((2, SUBLANES, tm), I32),
                        pltpu.VMEM((SUBLANES, LANES), I32), pltpu.SMEM((SUBLANES, LANES), I32),
                        pltpu.VMEM((ROW_TILES, LANES), F32),
                        pltpu.SemaphoreType.DMA((ROW_BUFS,)), pltpu.SemaphoreType.DMA((2,)),
                        pltpu.SemaphoreType.DMA(())],
        compiler_params=pltpu.CompilerParams(dimension_semantics=("arbitrary",), vmem_limit_bytes=VMEM_LIMIT,
                                             has_side_effects=True, disable_bounds_checks=True),
        name="mix_router",
    )(og, sb, ga, gt, x, w_o_gdn.astype(BF16), w_o_sconv.astype(BF16), w_mix_out.astype(BF16),
      norm_ffn_g.reshape(1, D_MODEL), wr, br)


def _moe_kernel(be_ref, nu_ref, xr_ref, x_ref, wgu_ref, bgu_ref, wd_ref, bd_ref, y_ref, wgu_bf, wd_bf):
    bm = x_ref.shape[0] // ROW_TILES
    i = pl.program_id(0)

    @pl.when(i < nu_ref[0])
    def _():
        @pl.when((i == 0) | (be_ref[i] != be_ref[jnp.maximum(i - 1, 0)]))
        def _():
            def cast_rows(r, carry):
                rows = pl.ds(pl.multiple_of(r * CAST_ROWS, CAST_ROWS), CAST_ROWS)
                wgu_bf[rows, :] = wgu_ref[rows, :].astype(BF16)
                wd_bf[rows, :] = wd_ref[rows, :].astype(BF16)
                return carry

            lax.fori_loop(0, D_MODEL // CAST_ROWS, cast_rows, 0)

        xb = jnp.concatenate([_load_row_tiles(x_ref, bm, s).astype(BF16) for s in range(ROW_TILES)], axis=1)
        gu = _dot(xb, wgu_bf[...]) + bgu_ref[...]
        gate = jnp.minimum(gu[:, :D_FF], SWIGLU_LIMIT)
        up = jnp.clip(gu[:, D_FF:], -SWIGLU_LIMIT, SWIGLU_LIMIT)
        hid = (up + 1.0) * gate * _sigmoid(SWIGLU_ALPHA * gate)
        _store_row_tiles(y_ref, _dot(hid.astype(BF16), wd_bf[...]) + bd_ref[...])


def _moe(xpad, block_e, n_used, x_block, n_slots, w_gate_up, b_gate_up, w_down, b_down, bm):
    nb = n_slots // bm

    def rows(i, be, nu, xr):
        return (jnp.minimum(i, nu[0] - 1), 0)

    def x_rows(i, be, nu, xr):
        return (xr[i], 0)

    def expert(i, be, nu, xr):
        return (be[i], 0, 0)

    grid_spec = pltpu.PrefetchScalarGridSpec(
        num_scalar_prefetch=3,
        grid=(nb,),
        in_specs=[pl.BlockSpec((bm * ROW_TILES, LANES), x_rows),
                  pl.BlockSpec((None, D_MODEL, 2 * D_FF), expert),
                  pl.BlockSpec((None, 1, 2 * D_FF), expert),
                  pl.BlockSpec((None, D_FF, D_MODEL), expert),
                  pl.BlockSpec((None, 1, D_MODEL), expert)],
        out_specs=pl.BlockSpec((bm * ROW_TILES, LANES), rows),
        scratch_shapes=[pltpu.VMEM((D_MODEL, 2 * D_FF), BF16), pltpu.VMEM((D_FF, D_MODEL), BF16)],
    )
    return pl.pallas_call(
        _moe_kernel,
        grid_spec=grid_spec,
        out_shape=jax.ShapeDtypeStruct((n_slots * ROW_TILES, LANES), F32),
        compiler_params=pltpu.CompilerParams(dimension_semantics=("arbitrary",), vmem_limit_bytes=VMEM_LIMIT),
        name="moe_mlp",
    )(block_e, n_used, x_block, xpad, w_gate_up, b_gate_up.reshape(N_EXPERTS, 1, 2 * D_FF),
      w_down, b_down.reshape(N_EXPERTS, 1, D_MODEL))


def _combine_kernel(dest_ref, dest_next_ref, h1_ref, tw_ref, gf_ref, ypad_ref, out_ref, buf_ref, sem):
    tm = h1_ref.shape[0]
    i = pl.program_id(0)
    cur = i % 2

    def gather(d_ref, parity):
        def issue(g, carry):
            for u in range(ISSUE_UNROLL):
                tok = g * (ISSUE_UNROLL // TOP_K) + u // TOP_K
                pltpu.make_async_copy(_row_tile(ypad_ref, d_ref[0, 0, g * ISSUE_UNROLL + u]),
                                      _row_tile(buf_ref.at[parity * TOP_K + u % TOP_K], tok),
                                      sem.at[parity]).start()
            return carry

        lax.fori_loop(0, tm * TOP_K // ISSUE_UNROLL, issue, 0)

    @pl.when(i == 0)
    def _():
        gather(dest_ref, 0)

    @pl.when(i + 1 < pl.num_programs(0))
    def _():
        gather(dest_next_ref, 1 - cur)

    for kk in range(TOP_K):
        pltpu.make_async_copy(ypad_ref.at[pl.ds(0, tm * ROW_TILES), :], buf_ref.at[cur * TOP_K + kk],
                              sem.at[cur]).wait()

    tw = tw_ref[...]
    accs = []
    ssq = jnp.zeros((tm, 1), F32)
    for s in range(ROW_TILES):
        acc = h1_ref[:, s * LANES:(s + 1) * LANES]
        for kk in range(TOP_K):
            acc = acc + tw[:, kk:kk + 1] * _load_row_tiles(buf_ref.at[cur * TOP_K + kk], tm, s)
        ssq = ssq + jnp.sum(acc * acc, axis=-1, keepdims=True)
        accs.append(acc)
    inv = lax.rsqrt(ssq * (1.0 / D_MODEL) + NORM_EPS)
    for s in range(ROW_TILES):
        out_ref[:, s * LANES:(s + 1) * LANES] = accs[s] * inv * gf_ref[:, s * LANES:(s + 1) * LANES]


def _combine(ypad, dest, h1, tw, norm_final_g):
    t = h1.shape[0]
    tm = min(TM_COMB, t)
    nt = t // tm
    dest3 = dest.reshape(nt, 1, tm * TOP_K)
    return pl.pallas_call(
        _combine_kernel,
        grid=(nt,),
        in_specs=[pl.BlockSpec((1, 1, tm * TOP_K), lambda i: (i, 0, 0), memory_space=pltpu.SMEM),
                  pl.BlockSpec((1, 1, tm * TOP_K), lambda i: (jnp.minimum(i + 1, nt - 1), 0, 0),
                               memory_space=pltpu.SMEM),
                  pl.BlockSpec((tm, D_MODEL), lambda i: (i, 0)),
                  pl.BlockSpec((tm, LANES), lambda i: (i, 0)),
                  pl.BlockSpec((1, D_MODEL), lambda i: (0, 0)),
                  pl.BlockSpec(memory_space=pl.ANY)],
        out_specs=pl.BlockSpec((tm, D_MODEL), lambda i: (i, 0)),
        out_shape=jax.ShapeDtypeStruct((t, D_MODEL), F32),
        scratch_shapes=[pltpu.VMEM((2 * TOP_K, tm * ROW_TILES, LANES), F32), pltpu.SemaphoreType.DMA((2,))],
        compiler_params=pltpu.CompilerParams(dimension_semantics=("arbitrary",), vmem_limit_bytes=VMEM_LIMIT,
                                             disable_bounds_checks=True),
        name="combine",
    )(dest3, dest3, h1, tw, norm_final_g.reshape(1, D_MODEL), ypad)


def _routing_tables(meta, cnt, t, bm, cap):
    counts = cnt[0, :N_EXPERTS].astype(I32)
    pcounts = ((counts + bm - 1) // bm) * bm
    pend = jnp.cumsum(pcounts)
    pstart = (pend - pcounts).astype(I32)
    idx, rank = meta[:, :TOP_K, :], meta[:, TOP_K:, :]
    onehot = idx[..., None] == jnp.arange(N_EXPERTS, dtype=I32)
    dest = jnp.sum(jnp.where(onehot, pstart, 0), axis=-1).astype(I32) + rank
    dest = dest.transpose(0, 2, 1).reshape(-1)
    n_slots = ((t * TOP_K + N_EXPERTS * (bm - 1) + bm - 1) // bm) * bm
    nb = n_slots // bm
    n_used = pend[-1] // bm
    starts = jnp.arange(nb, dtype=I32) * bm
    block_e = jnp.minimum(jnp.sum((pend[None, :] <= starts[:, None]).astype(I32), axis=1), N_EXPERTS - 1)
    x_block = block_e * (cap // bm) + (starts - pstart[block_e]) // bm
    last = jnp.maximum(n_used - 1, 0)
    used = jnp.arange(nb) < n_used
    block_e = jnp.where(used, block_e, block_e[last]).astype(I32)
    x_block = jnp.where(used, x_block, x_block[last]).astype(I32)
    return dest, block_e, n_used.reshape(1).astype(I32), x_block, n_slots


def kernel(x, norm_mix_g, w_in, gdn_conv_w, gdn_A_log, gdn_dt_bias, gdn_norm_g, w_o_gdn, sconv_w, w_o_sconv,
           w_mix_out, norm_ffn_g, w_router, b_router, w_gate_up, b_gate_up, w_down, b_down, norm_final_g):
    bsz, seq, _ = x.shape
    t = bsz * seq
    assert norm_mix_g.shape[0] == 1, "single-layer block only"
    q, k, v, zs, gbeta, sb, ga, gt = _inproj(x, norm_mix_g[0], w_in[0], gdn_conv_w[0], gdn_A_log[0],
                                             gdn_dt_bias[0], sconv_w[0])
    og = _gdn(q, k, v, zs, gbeta, gdn_norm_g[0])

    def flat(a):
        return a.reshape(t, a.shape[-1])

    bm = BM_MOE
    cap = pl.cdiv(t, bm) * bm
    h1, meta, tw, cnt, xpad = _mix(flat(og), flat(sb), flat(ga), flat(gt), flat(x), w_o_gdn[0], w_o_sconv[0],
                                   w_mix_out[0], norm_ffn_g[0], w_router[0], b_router[0], cap, bm)
    dest, block_e, n_used, x_block, n_slots = _routing_tables(meta, cnt, t, bm, cap)
    ypad = _moe(xpad, block_e, n_used, x_block, n_slots, w_gate_up[0], b_gate_up[0], w_down[0], b_down[0], bm)
    return _combine(ypad, dest, h1, tw, norm_final_g).reshape(bsz, seq, D_MODEL)
```

```python
import functools

import jax
import jax.numpy as jnp
from jax import lax
from jax.experimental import pallas as pl
from jax.experimental.pallas import tpu as pltpu

F32 = jnp.float32
BF16 = jnp.bfloat16
I32 = jnp.int32

D_MODEL = 1024
HEADS = 8
HEAD_DIM = 128
GDN_WIDTH = HEADS * HEAD_DIM
GDN_CONV = 4
CHUNK = 128
INV_BLOCK = 32
SC_CONV = 3
N_EXPERTS = 32
TOP_K = 4
D_FF = 1024
SWIGLU_LIMIT = 7.0
SWIGLU_ALPHA = 1.702
NORM_EPS = 1e-6

LANES = 128
SUBLANES = 8
VMEM_LIMIT = 56 * 1024 * 1024

TM_IN = 512
CT_IN = 256
TS_GDN = 512
NB_GDN = 2
TM_MIX = 512
ROW_BUFS = 3
MIX_SPLIT = 4
BM_MOE = 512
TM_COMB = 256
ISSUE_UNROLL = 8
CAST_ROWS = 128


NEG_LOG2E = -1.4426950408889634


def _sigmoid(x):
    return 1.0 / (1.0 + jnp.exp2(x * NEG_LOG2E))


def _silu(x):
    return x * _sigmoid(x)


def _softplus(x):
    return jnp.maximum(x, 0.0) + jnp.log(1.0 + jnp.exp(-jnp.abs(x)))


def _rms_norm(x, g):
    return x * lax.rsqrt(jnp.mean(x * x, axis=-1, keepdims=True) + NORM_EPS) * g


def _dot(a, b):
    return jnp.dot(a, b, preferred_element_type=F32)


def _dot_nt(a, b):
    return lax.dot_general(a, b, (((1,), (1,)), ((), ())), preferred_element_type=F32)


def _dot_tn(a, b):
    return lax.dot_general(a, b, (((0,), (0,)), ((), ())), preferred_element_type=F32)


ROW_TILES = D_MODEL // LANES


def _store_row_tiles(ref, val):
    m = val.shape[0]
    for s in range(ROW_TILES):
        ref[pl.ds(s, m, stride=ROW_TILES), :] = val[:, s * LANES:(s + 1) * LANES]


def _load_row_tiles(ref, m, s):
    return ref[pl.ds(s, m, stride=ROW_TILES), :]


def _row_tile(ref, row):
    return ref.at[pl.ds(pl.multiple_of(row * ROW_TILES, ROW_TILES), ROW_TILES), :]


def _causal_conv(p, prev, w, stage_ref):
    taps = w.shape[0]
    tm = p.shape[0]
    stage_ref[0:SUBLANES, :] = prev
    stage_ref[SUBLANES:SUBLANES + tm, :] = p
    y = w[taps - 1:taps] * p
    for s in range(1, taps):
        y = y + w[taps - 1 - s:taps - s] * stage_ref[SUBLANES - s:SUBLANES - s + tm, :]
    return y


def _inproj_kernel(x_ref, ng_ref, wqkv_ref, wz_ref, wab_ref, wxs_ref, wgb_ref, wgc_ref, wga_ref, wgt_ref,
                   cw_ref, scw_ref, alog_ref, dtb_ref,
                   q_ref, k_ref, v_ref, zs_ref, gbeta_ref, sb_ref, ga_ref, gt_ref,
                   carry_qkv, carry_u, xn_scr, stage_ref):
    tm = x_ref.shape[0]

    @pl.when(pl.program_id(1) == 0)
    def _():
        carry_qkv[...] = jnp.zeros_like(carry_qkv)
        carry_u[...] = jnp.zeros_like(carry_u)

    x = x_ref[...]
    xn_scr[...] = _rms_norm(x, ng_ref[...]).astype(BF16)

    q_scale = HEAD_DIM ** -0.5
    outs = (q_ref, k_ref, v_ref)
    for ci in range(3 * GDN_WIDTH // CT_IN):
        c0 = ci * CT_IN
        p = _dot(xn_scr[...], wqkv_ref[:, c0:c0 + CT_IN])
        y = _causal_conv(p, carry_qkv[:, c0:c0 + CT_IN], cw_ref[:, c0:c0 + CT_IN], stage_ref.at[ci % 2])
        carry_qkv[:, c0:c0 + CT_IN] = p[tm - SUBLANES:tm]
        y = _silu(y)
        which = c0 // GDN_WIDTH
        o0 = c0 - which * GDN_WIDTH
        for hh in range(CT_IN // HEAD_DIM):
            yh = y[:, hh * HEAD_DIM:(hh + 1) * HEAD_DIM]
            if which < 2:
                yh = yh * lax.rsqrt(jnp.sum(yh * yh, axis=-1, keepdims=True) + NORM_EPS)
            if which == 0:
                yh = yh * q_scale
            outs[which][:, o0 + hh * HEAD_DIM:o0 + (hh + 1) * HEAD_DIM] = yh.astype(BF16)

    for ci in range(D_MODEL // CT_IN):
        c0 = ci * CT_IN
        zs_ref[:, c0:c0 + CT_IN] = _silu(_dot(xn_scr[...], wz_ref[:, c0:c0 + CT_IN])).astype(BF16)
        ga_ref[:, c0:c0 + CT_IN] = _sigmoid(_dot(xn_scr[...], wga_ref[:, c0:c0 + CT_IN])).astype(BF16)
        gt_ref[:, c0:c0 + CT_IN] = _sigmoid(_dot(xn_scr[...], wgt_ref[:, c0:c0 + CT_IN])).astype(BF16)

    for ci in range(D_MODEL // CT_IN):
        c0 = ci * CT_IN
        xs = _dot(xn_scr[...], wxs_ref[:, c0:c0 + CT_IN])
        gc = _dot(xn_scr[...], wgc_ref[:, c0:c0 + CT_IN])
        u = gc * xs
        y = _causal_conv(u, carry_u[:, c0:c0 + CT_IN], scw_ref[:, c0:c0 + CT_IN], stage_ref.at[ci % 2])
        carry_u[:, c0:c0 + CT_IN] = u[tm - SUBLANES:tm]
        gb = _dot(xn_scr[...], wgb_ref[:, c0:c0 + CT_IN])
        sb_ref[:, c0:c0 + CT_IN] = (gb * y).astype(BF16)

    ab = _dot(xn_scr[...], wab_ref[...])
    lane = lax.broadcasted_iota(I32, ab.shape, 1)
    g = -jnp.exp(alog_ref[...]) * _softplus(ab + dtb_ref[...])
    g_hi = g.astype(BF16).astype(F32)
    gbeta_ref[...] = jnp.where(lane < HEADS, g_hi, jnp.where(lane < 2 * HEADS, _sigmoid(ab), g - g_hi))


def _inproj(x, norm_g, w_in, conv_w, a_log, dt_bias, sconv_w):
    bsz, seq, _ = x.shape
    tm = min(TM_IN, seq)
    o = 0
    w = {}
    for name, width in (("qkv", 3 * GDN_WIDTH), ("z", GDN_WIDTH), ("a", HEADS), ("b", HEADS),
                        ("xs", D_MODEL), ("gb", D_MODEL), ("gc", D_MODEL), ("ga", D_MODEL), ("gt", D_MODEL)):
        w[name] = w_in[:, o:o + width]
        o += width
    wab = (jnp.zeros((D_MODEL, LANES), F32).at[:, :HEADS].set(w["a"]).at[:, HEADS:2 * HEADS].set(w["b"])
           .at[:, 2 * HEADS:3 * HEADS].set(w["a"]))
    alog = jnp.zeros((1, LANES), F32).at[0, :HEADS].set(a_log).at[0, 2 * HEADS:3 * HEADS].set(a_log)
    dtb = jnp.zeros((1, LANES), F32).at[0, :HEADS].set(dt_bias).at[0, 2 * HEADS:3 * HEADS].set(dt_bias)
    weights = [w["qkv"], w["z"], wab, w["xs"], w["gb"], w["gc"], w["ga"], w["gt"]]
    weights = [a.astype(BF16) for a in weights]

    def const(shape):
        return pl.BlockSpec(shape, lambda b, s: (0,) * len(shape), pipeline_mode=pl.Buffered(1))

    def tok(width):
        return pl.BlockSpec((None, tm, width), lambda b, s: (b, s, 0))

    out_bf = jax.ShapeDtypeStruct((bsz, seq, D_MODEL), BF16)
    return pl.pallas_call(
        _inproj_kernel,
        grid=(bsz, seq // tm),
        in_specs=[tok(D_MODEL), const((1, D_MODEL))] + [const(a.shape) for a in weights]
        + [const((GDN_CONV, 3 * GDN_WIDTH)), const((SC_CONV, D_MODEL)), const((1, LANES)), const((1, LANES))],
        out_specs=[tok(D_MODEL)] * 4 + [tok(LANES)] + [tok(D_MODEL)] * 3,
        out_shape=[out_bf] * 4 + [jax.ShapeDtypeStruct((bsz, seq, LANES), F32)] + [out_bf] * 3,
        scratch_shapes=[pltpu.VMEM((SUBLANES, 3 * GDN_WIDTH), F32), pltpu.VMEM((SUBLANES, D_MODEL), F32),
                        pltpu.VMEM((tm, D_MODEL), BF16), pltpu.VMEM((2, SUBLANES + tm, CT_IN), F32)],
        compiler_params=pltpu.CompilerParams(dimension_semantics=("arbitrary", "arbitrary"),
                                             vmem_limit_bytes=VMEM_LIMIT),
        name="inproj",
    )(x, norm_g.reshape(1, D_MODEL), *weights, conv_w, sconv_w, alog, dtb)


def _gdn_kernel(q_ref, k_ref, v_ref, zs_ref, gbeta_ref, ng_ref, o_ref, state_ref):
    nb, ts = q_ref.shape[0], q_ref.shape[1]

    @pl.when(pl.program_id(1) == 0)
    def _():
        state_ref[...] = jnp.zeros_like(state_ref)

    ri = lax.broadcasted_iota(I32, (CHUNK, CHUNK), 0)
    ci = lax.broadcasted_iota(I32, (CHUNK, CHUNK), 1)
    causal = ri >= ci
    strict = ri > ci
    ltri = causal.astype(BF16)
    eye = (ri == ci).astype(F32)
    sh = INV_BLOCK.bit_length() - 1
    diag_blk = (ri >> sh) == (ci >> sh)
    off_blks = []
    while (1 << sh) < CHUNK:
        off_blks.append(((ri >> (sh + 1)) == (ci >> (sh + 1))) & ((ri >> sh) != (ci >> sh)) & strict)
        sh += 1
    ng = ng_ref[...]
    hs = range(nb * HEADS)
    bi = [u // HEADS for u in hs]
    hi = [u % HEADS for u in hs]

    def chunk(c, carry):
        rows = pl.ds(pl.multiple_of(c * CHUNK, CHUNK), CHUNK)
        cols = [slice(hi[h] * HEAD_DIM, (hi[h] + 1) * HEAD_DIM) for h in hs]
        gbc = [gbeta_ref[b, rows, :] for b in range(nb)]
        gcum = [_dot(ltri, gbc[b].astype(BF16)) for b in range(nb)]
        gcum_t = [gcum[b].T for b in range(nb)]
        gc = [gcum[bi[h]][:, hi[h]:hi[h] + 1] + gcum[bi[h]][:, 2 * HEADS + hi[h]:2 * HEADS + hi[h] + 1]
              for h in hs]
        gr = [gcum_t[bi[h]][hi[h]:hi[h] + 1, :] + gcum_t[bi[h]][2 * HEADS + hi[h]:2 * HEADS + hi[h] + 1, :]
              for h in hs]
        beta = [gbc[bi[h]][:, HEADS + hi[h]:HEADS + hi[h] + 1] for h in hs]
        g_last = [gc[h][CHUNK - 1:CHUNK, :] for h in hs]
        decay = [jnp.where(causal, jnp.exp(jnp.where(causal, gc[h] - gr[h], 0.0)), 0.0) for h in hs]
        eg = [jnp.exp(gc[h]) for h in hs]

        kf = [k_ref[bi[h], rows, cols[h]].astype(F32) for h in hs]
        kb = [kf[h] * beta[h] for h in hs]
        kt = [kf[h].T for h in hs]
        ktb = [kt[h].astype(BF16) for h in hs]
        lhs = [jnp.concatenate([kb[h].astype(BF16), q_ref[bi[h], rows, cols[h]]], axis=0) for h in hs]
        aq = [_dot(lhs[h], ktb[h]) for h in hs]
        aqk = [(aq[h][CHUNK:] * decay[h]).astype(BF16) for h in hs]
        lm = [jnp.where(strict, aq[h][:CHUNK] * decay[h], 0.0) for h in hs]
        m = [jnp.where(diag_blk, -lm[h], 0.0) for h in hs]
        tinv = [eye + m[h] for h in hs]
        mb = [m[h].astype(BF16) for h in hs]
        m = [_dot(mb[h], mb[h]) for h in hs]
        levels = INV_BLOCK.bit_length() - 1
        for j in range(1, levels):
            mb = [m[h].astype(BF16) for h in hs]
            if j < levels - 1:
                st = [_dot(jnp.concatenate([tinv[h].astype(BF16), mb[h]], axis=0), mb[h]) for h in hs]
                tinv = [tinv[h] + st[h][:CHUNK] for h in hs]
                m = [st[h][CHUNK:] for h in hs]
            else:
                tinv = [tinv[h] + _dot(tinv[h].astype(BF16), mb[h]) for h in hs]
        for off in off_blks:
            tb = [tinv[h].astype(BF16) for h in hs]
            ct = [_dot(jnp.where(off, lm[h], 0.0).astype(BF16), tb[h]) for h in hs]
            tinv = [tinv[h] - _dot(tb[h], ct[h].astype(BF16)) for h in hs]
        rhs = [jnp.concatenate([(v_ref[bi[h], rows, cols[h]].astype(F32) * beta[h]).astype(BF16),
                                (kb[h] * eg[h]).astype(BF16)], axis=1) for h in hs]
        uw = [_dot(tinv[h].astype(BF16), rhs[h]) for h in hs]

        sts = [state_ref[h] for h in hs]
        stb = [sts[h].astype(BF16) for h in hs]
        wq = [jnp.concatenate([uw[h][:, HEAD_DIM:].astype(BF16),
                               (q_ref[bi[h], rows, cols[h]].astype(F32) * eg[h]).astype(BF16)], axis=0) for h in hs]
        ws = [_dot(wq[h], stb[h]) for h in hs]
        vnb = [(uw[h][:, :HEAD_DIM] - ws[h][:CHUNK]).astype(BF16) for h in hs]
        kdt = [(kt[h] * jnp.exp(g_last[h] - gr[h])).astype(BF16) for h in hs]
        o = [ws[h][CHUNK:] + _dot(aqk[h], vnb[h]) for h in hs]
        for h in hs:
            state_ref[h] = sts[h] * jnp.exp(g_last[h]) + _dot(kdt[h], vnb[h])
        for h in hs:
            oh = _rms_norm(o[h], ng) * zs_ref[bi[h], rows, cols[h]].astype(F32)
            o_ref[bi[h], rows, cols[h]] = oh.astype(BF16)
        return carry

    lax.fori_loop(0, ts // CHUNK, chunk, 0)


def _gdn(q, k, v, zs, gbeta, norm_g):
    bsz, seq, _ = q.shape
    ts = min(TS_GDN, seq)
    nb = NB_GDN if bsz % NB_GDN == 0 else 1

    def tok(width):
        return pl.BlockSpec((nb, ts, width), lambda b, s: (b, s, 0))

    return pl.pallas_call(
        _gdn_kernel,
        grid=(bsz // nb, seq // ts),
        in_specs=[tok(GDN_WIDTH)] * 4 + [tok(LANES), pl.BlockSpec((1, HEAD_DIM), lambda b, s: (0, 0))],
        out_specs=tok(GDN_WIDTH),
        out_shape=jax.ShapeDtypeStruct((bsz, seq, GDN_WIDTH), BF16),
        scratch_shapes=[pltpu.VMEM((nb * HEADS, HEAD_DIM, HEAD_DIM), F32)],
        compiler_params=pltpu.CompilerParams(dimension_semantics=("arbitrary", "arbitrary"),
                                             vmem_limit_bytes=VMEM_LIMIT),
        name="gdn",
    )(q, k, v, zs, gbeta, norm_g.reshape(1, HEAD_DIM))


def _mix_kernel(og_ref, sb_ref, ga_ref, gt_ref, x_ref, wog_ref, wos_ref, wmix_ref, gffn_ref, wr_ref, br_ref,
                h1_ref, meta_ref, tw_ref, cnt_ref, xpad_ref,
                carry_ref, xt_ref, dvm_ref, dsm_ref, cvm_ref, csm_ref, zero_ref, sem_rows, sem_s, sem_z,
                *, nt, cap, bm):
    tm = x_ref.shape[0]
    i = pl.program_id(0)
    slot = i % ROW_BUFS

    def wait_rows(s):
        for _ in range(TOP_K):
            pltpu.make_async_copy(xt_ref.at[s], xpad_ref.at[pl.ds(0, tm * ROW_TILES), :], sem_rows.at[s]).wait()

    def slots_copy(par):
        return pltpu.make_async_copy(dvm_ref.at[par], dsm_ref.at[par], sem_s.at[par])

    def issue_rows(s, par, half=None):
        def issue(g, carry):
            for u in range(ISSUE_UNROLL):
                tok = g * (ISSUE_UNROLL // TOP_K) + u // TOP_K
                pltpu.make_async_copy(_row_tile(xt_ref.at[s], tok),
                                      _row_tile(xpad_ref, dsm_ref[par, u % TOP_K, tok]), sem_rows.at[s]).start()
            return carry

        groups = tm * TOP_K // ISSUE_UNROLL
        lo, hi = (0, groups) if half is None else (half * (groups // 2), (half + 1) * (groups // 2))
        lax.fori_loop(lo, hi, issue, 0)

    par = i % 2

    def issue_prev(half):
        @pl.when(i >= 1)
        def _():
            if half == 0:
                slots_copy(1 - par).wait()
            issue_rows((i + ROW_BUFS - 1) % ROW_BUFS, 1 - par, half)

    @pl.when(i == 0)
    def _():
        carry_ref[...] = jnp.zeros_like(carry_ref)

    issue_prev(0)

    @pl.when(i >= ROW_BUFS)
    def _():
        wait_rows(slot)

    sub = tm // MIX_SPLIT
    rs = [slice(r * sub, (r + 1) * sub) for r in range(MIX_SPLIT)]
    ya = [_dot(og_ref[r, :], wog_ref[...]) for r in rs]
    yb = [_dot(sb_ref[r, :], wos_ref[...]) for r in rs]
    merged = [(ga_ref[r, :].astype(F32) * a + gt_ref[r, :].astype(F32) * b).astype(BF16)
              for r, a, b in zip(rs, ya, yb)]
    h1 = [x_ref[r, :] + _dot(mg, wmix_ref[...]) for r, mg in zip(rs, merged)]
    xn2 = [_rms_norm(h, gffn_ref[...]) for h in h1]
    x_hi = [v.astype(BF16) for v in xn2]
    x_lo = [(v - hi.astype(F32)).astype(BF16) for v, hi in zip(xn2, x_hi)]
    parts = [_dot(hi, wr_ref[...]) + _dot(lo, wr_ref[...]) for hi, lo in zip(x_hi, x_lo)]
    for r, h in zip(rs, h1):
        h1_ref[r, :] = h
    _store_row_tiles(xt_ref.at[slot], jnp.concatenate(xn2, axis=0))
    logits =jnp.concatenate([p[:, :LANES] + p[:, LANES:] for p in parts], axis=0) + br_ref[...]

    lane = lax.broadcasted_iota(I32, (tm, LANES), 1)
    lane_f = lane.astype(F32)
    neg_inf = jnp.float32(-jnp.inf)
    work = jnp.where(lane < N_EXPERTS, logits, neg_inf)
    hits, vals = [], []
    for _ in range(TOP_K):
        m = jnp.max(work, axis=-1, keepdims=True)
        first = jnp.min(jnp.where(work == m, lane_f, float(LANES)), axis=-1, keepdims=True)
        hit = lane_f == first
        work = jnp.where(hit, neg_inf, work)
        hits.append(hit)
        vals.append((m, first))
    exps = [jnp.exp(m - vals[0][0]) for m, _ in vals]
    denom = exps[0] + exps[1] + exps[2] + exps[3]

    sel = jnp.zeros((tm, LANES), F32)
    for hit in hits:
        sel = sel + hit.astype(F32)
    ri = lax.broadcasted_iota(I32, (tm, tm), 0)
    ci = lax.broadcasted_iota(I32, (tm, tm), 1)
    before = (ri > ci).astype(BF16)
    rank_mat = _dot(before, sel.astype(BF16)) + carry_ref[...]
    carry_ref[...] = carry_ref[...] + jnp.sum(sel, axis=0, keepdims=True)
    cnt_ref[...] = carry_ref[...]

    meta = jnp.zeros((tm, LANES), F32)
    rows = jnp.zeros((tm, LANES), F32)
    tw_out = jnp.zeros((tm, LANES), F32)
    for kk in range(TOP_K):
        rk = jnp.sum(jnp.where(hits[kk], rank_mat, 0.0), axis=-1, keepdims=True)
        meta = jnp.where(lane == kk, vals[kk][1], meta)
        meta = jnp.where(lane == TOP_K + kk, rk, meta)
        rows = jnp.where(lane == kk, vals[kk][1] * float(cap) + rk, rows)
        tw_out = jnp.where(lane == kk, exps[kk] / denom, tw_out)
    meta_ref[0] = meta.T[0:2 * TOP_K, :].astype(I32)
    tw_ref[...] = tw_out

    dvm_ref[par] = rows.T[0:SUBLANES, :].astype(I32)
    slots_copy(par).start()
    issue_prev(1)

    @pl.when(i == nt - 1)
    def _():
        slots_copy(par).wait()
        issue_rows(slot, par)
        for back in range(min(ROW_BUFS, nt)):
            wait_rows((i + ROW_BUFS - back) % ROW_BUFS)
        cvm_ref[...] = jnp.broadcast_to(carry_ref[...], cvm_ref.shape).astype(I32)
        counts_to_smem = pltpu.make_async_copy(cvm_ref, csm_ref, sem_z)
        counts_to_smem.start()
        counts_to_smem.wait()
        zero_ref[...] = jnp.zeros_like(zero_ref)
        for e in range(N_EXPERTS):
            count = csm_ref[0, e]
            padded = (count + (bm - 1)) // bm * bm

            def zero_row(r, carry):
                pltpu.make_async_copy(zero_ref, _row_tile(xpad_ref, e * cap + r), sem_z).start()
                return carry

            def zero_wait(r, carry):
                pltpu.make_async_copy(zero_ref, _row_tile(xpad_ref, 0), sem_z).wait()
                return carry

            lax.fori_loop(count, padded, zero_row, 0)
            lax.fori_loop(count, padded, zero_wait, 0)


def _mix(og, sb, ga, gt, x, w_o_gdn, w_o_sconv, w_mix_out, norm_ffn_g, w_router, b_router, cap, bm):
    t = x.shape[0]
    tm = min(TM_MIX, t)
    nt = t // tm
    assert N_EXPERTS * cap < 2 ** 24, "row indices are formed in f32"
    wr = jnp.zeros((D_MODEL, LANES), F32).at[:, :N_EXPERTS].set(w_router)
    wr_hi = wr.astype(BF16)
    wr = jnp.concatenate([wr_hi, (wr - wr_hi.astype(F32)).astype(BF16)], axis=1)
    br = jnp.zeros((1, LANES), F32).at[0, :N_EXPERTS].set(b_router)

    def const(shape):
        return pl.BlockSpec(shape, lambda i: (0,) * len(shape), pipeline_mode=pl.Buffered(1))

    def tok(rows, width):
        return pl.BlockSpec((rows, width), lambda i: (i, 0))

    return pl.pallas_call(
        functools.partial(_mix_kernel, nt=nt, cap=cap, bm=bm),
        grid=(nt,),
        in_specs=[tok(tm, D_MODEL)] * 5 + [const((D_MODEL, D_MODEL))] * 3
        + [const((1, D_MODEL)), const((D_MODEL, 2 * LANES)), const((1, LANES))],
        out_specs=[tok(tm, D_MODEL), pl.BlockSpec((1, 2 * TOP_K, tm), lambda i: (i, 0, 0)), tok(tm, LANES),
                   pl.BlockSpec((1, LANES), lambda i: (0, 0)), pl.BlockSpec(memory_space=pl.ANY)],
        out_shape=[jax.ShapeDtypeStruct((t, D_MODEL), F32), jax.ShapeDtypeStruct((nt, 2 * TOP_K, tm), I32),
                   jax.ShapeDtypeStruct((t, LANES), F32), jax.ShapeDtypeStruct((1, LANES), F32),
                   jax.ShapeDtypeStruct((N_EXPERTS * cap * ROW_TILES, LANES), F32)],
        scratch_shapes=[pltpu.VMEM((1, LANES), F32), pltpu.VMEM((ROW_BUFS, tm * ROW_TILES, LANES), F32),
                        pltpu.VMEM((2, SUBLANES, tm), I32), pltpu.SMEM((2, SUBLANES, tm), I32),
                        pltpu.VMEM((SUBLANES, LANES), I32), pltpu.SMEM((SUBLANES, LANES), I32),
                        pltpu.VMEM((ROW_TILES, LANES), F32),
                        pltpu.SemaphoreType.DMA((ROW_BUFS,)), pltpu.SemaphoreType.DMA((2,)),
                        pltpu.SemaphoreType.DMA(())],
        compiler_params=pltpu.CompilerParams(dimension_semantics=("arbitrary",), vmem_limit_bytes=VMEM_LIMIT,
                                             has_side_effects=True, disable_bounds_checks=True),
        name="mix_router",
    )(og, sb, ga, gt, x, w_o_gdn.astype(BF16), w_o_sconv.astype(BF16), w_mix_out.astype(BF16),
      norm_ffn_g.reshape(1, D_MODEL), wr, br)


def _moe_kernel(be_ref, nu_ref, xr_ref, x_ref, wgu_ref, bgu_ref, wd_ref, bd_ref, y_ref, wgu_bf, wd_bf):
    bm = x_ref.shape[0] // ROW_TILES
    i = pl.program_id(0)

    @pl.when(i < nu_ref[0])
    def _():
        @pl.when((i == 0) | (be_ref[i] != be_ref[jnp.maximum(i - 1, 0)]))
        def _():
            def cast_rows(r, carry):
                rows = pl.ds(pl.multiple_of(r * CAST_ROWS, CAST_ROWS), CAST_ROWS)
                wgu_bf[rows, :] = wgu_ref[rows, :].astype(BF16)
                wd_bf[rows, :] = wd_ref[rows, :].astype(BF16)
                return carry

            lax.fori_loop(0, D_MODEL // CAST_ROWS, cast_rows, 0)

        xb = jnp.concatenate([_load_row_tiles(x_ref, bm, s).astype(BF16) for s in range(ROW_TILES)], axis=1)
        gu = _dot(xb, wgu_bf[...]) + bgu_ref[...]
        gate = jnp.minimum(gu[:, :D_FF], SWIGLU_LIMIT)
        up = jnp.clip(gu[:, D_FF:], -SWIGLU_LIMIT, SWIGLU_LIMIT)
        hid = (up + 1.0) * gate * _sigmoid(SWIGLU_ALPHA * gate)
        _store_row_tiles(y_ref, _dot(hid.astype(BF16), wd_bf[...]) + bd_ref[...])


def _moe(xpad, block_e, n_used, x_block, n_slots, w_gate_up, b_gate_up, w_down, b_down, bm):
    nb = n_slots // bm

    def rows(i, be, nu, xr):
        return (jnp.minimum(i, nu[0] - 1), 0)

    def x_rows(i, be, nu, xr):
        return (xr[i], 0)

    def expert(i, be, nu, xr):
        return (be[i], 0, 0)

    grid_spec = pltpu.PrefetchScalarGridSpec(
        num_scalar_prefetch=3,
        grid=(nb,),
        in_specs=[pl.BlockSpec((bm * ROW_TILES, LANES), x_rows),
                  pl.BlockSpec((None, D_MODEL, 2 * D_FF), expert),
                  pl.BlockSpec((None, 1, 2 * D_FF), expert),
                  pl.BlockSpec((None, D_FF, D_MODEL), expert),
                  pl.BlockSpec((None, 1, D_MODEL), expert)],
        out_specs=pl.BlockSpec((bm * ROW_TILES, LANES), rows),
        scratch_shapes=[pltpu.VMEM((D_MODEL, 2 * D_FF), BF16), pltpu.VMEM((D_FF, D_MODEL), BF16)],
    )
    return pl.pallas_call(
        _moe_kernel,
        grid_spec=grid_spec,
        out_shape=jax.ShapeDtypeStruct((n_slots * ROW_TILES, LANES), F32),
        compiler_params=pltpu.CompilerParams(dimension_semantics=("arbitrary",), vmem_limit_bytes=VMEM_LIMIT),
        name="moe_mlp",
    )(block_e, n_used, x_block, xpad, w_gate_up, b_gate_up.reshape(N_EXPERTS, 1, 2 * D_FF),
      w_down, b_down.reshape(N_EXPERTS, 1, D_MODEL))


def _combine_kernel(dest_ref, dest_next_ref, h1_ref, tw_ref, gf_ref, ypad_ref, out_ref, buf_ref, sem):
    tm = h1_ref.shape[0]
    i = pl.program_id(0)
    cur = i % 2

    def gather(d_ref, parity):
        def issue(g, carry):
            for u in range(ISSUE_UNROLL):
                tok = g * (ISSUE_UNROLL // TOP_K) + u // TOP_K
                pltpu.make_async_copy(_row_tile(ypad_ref, d_ref[0, 0, g * ISSUE_UNROLL + u]),
                                      _row_tile(buf_ref.at[parity * TOP_K + u % TOP_K], tok),
                                      sem.at[parity]).start()
            return carry

        lax.fori_loop(0, tm * TOP_K // ISSUE_UNROLL, issue, 0)

    @pl.when(i == 0)
    def _():
        gather(dest_ref, 0)

    @pl.when(i + 1 < pl.num_programs(0))
    def _():
        gather(dest_next_ref, 1 - cur)

    for kk in range(TOP_K):
        pltpu.make_async_copy(ypad_ref.at[pl.ds(0, tm * ROW_TILES), :], buf_ref.at[cur * TOP_K + kk],
                              sem.at[cur]).wait()

    tw = tw_ref[...]
    accs = []
    ssq = jnp.zeros((tm, 1), F32)
    for s in range(ROW_TILES):
        acc = h1_ref[:, s * LANES:(s + 1) * LANES]
        for kk in range(TOP_K):
            acc = acc + tw[:, kk:kk + 1] * _load_row_tiles(buf_ref.at[cur * TOP_K + kk], tm, s)
        ssq = ssq + jnp.sum(acc * acc, axis=-1, keepdims=True)
        accs.append(acc)
    inv = lax.rsqrt(ssq * (1.0 / D_MODEL) + NORM_EPS)
    for s in range(ROW_TILES):
        out_ref[:, s * LANES:(s + 1) * LANES] = accs[s] * inv * gf_ref[:, s * LANES:(s + 1) * LANES]


def _combine(ypad, dest, h1, tw, norm_final_g):
    t = h1.shape[0]
    tm = min(TM_COMB, t)
    nt = t // tm
    dest3 = dest.reshape(nt, 1, tm * TOP_K)
    return pl.pallas_call(
        _combine_kernel,
        grid=(nt,),
        in_specs=[pl.BlockSpec((1, 1, tm * TOP_K), lambda i: (i, 0, 0), memory_space=pltpu.SMEM),
                  pl.BlockSpec((1, 1, tm * TOP_K), lambda i: (jnp.minimum(i + 1, nt - 1), 0, 0),
                               memory_space=pltpu.SMEM),
                  pl.BlockSpec((tm, D_MODEL), lambda i: (i, 0)),
                  pl.BlockSpec((tm, LANES), lambda i: (i, 0)),
                  pl.BlockSpec((1, D_MODEL), lambda i: (0, 0)),
                  pl.BlockSpec(memory_space=pl.ANY)],
        out_specs=pl.BlockSpec((tm, D_MODEL), lambda i: (i, 0)),
        out_shape=jax.ShapeDtypeStruct((t, D_MODEL), F32),
        scratch_shapes=[pltpu.VMEM((2 * TOP_K, tm * ROW_TILES, LANES), F32), pltpu.SemaphoreType.DMA((2,))],
        compiler_params=pltpu.CompilerParams(dimension_semantics=("arbitrary",), vmem_limit_bytes=VMEM_LIMIT,
                                             disable_bounds_checks=True),
        name="combine",
    )(dest3, dest3, h1, tw, norm_final_g.reshape(1, D_MODEL), ypad)


def _routing_tables(meta, cnt, t, bm, cap):
    counts = cnt[0, :N_EXPERTS].astype(I32)
    pcounts = ((counts + bm - 1) // bm) * bm
    pend = jnp.cumsum(pcounts)
    pstart = (pend - pcounts).astype(I32)
    idx, rank = meta[:, :TOP_K, :], meta[:, TOP_K:, :]
    onehot = idx[..., None] == jnp.arange(N_EXPERTS, dtype=I32)
    dest = jnp.sum(jnp.where(onehot, pstart, 0), axis=-1).astype(I32) + rank
    dest = dest.transpose(0, 2, 1).reshape(-1)
    n_slots = ((t * TOP_K + N_EXPERTS * (bm - 1) + bm - 1) // bm) * bm
    nb = n_slots // bm
    n_used = pend[-1] // bm
    starts = jnp.arange(nb, dtype=I32) * bm
    block_e = jnp.minimum(jnp.sum((pend[None, :] <= starts[:, None]).astype(I32), axis=1), N_EXPERTS - 1)
    x_block = block_e * (cap // bm) + (starts - pstart[block_e]) // bm
    last = jnp.maximum(n_used - 1, 0)
    used = jnp.arange(nb) < n_used
    block_e = jnp.where(used, block_e, block_e[last]).astype(I32)
    x_block = jnp.where(used, x_block, x_block[last]).astype(I32)
    return dest, block_e, n_used.reshape(1).astype(I32), x_block, n_slots


def kernel(x, norm_mix_g, w_in, gdn_conv_w, gdn_A_log, gdn_dt_bias, gdn_norm_g, w_o_gdn, sconv_w, w_o_sconv,
           w_mix_out, norm_ffn_g, w_router, b_router, w_gate_up, b_gate_up, w_down, b_down, norm_final_g):
    bsz, seq, _ = x.shape
    t = bsz * seq
    assert norm_mix_g.shape[0] == 1, "single-layer block only"
    q, k, v, zs, gbeta, sb, ga, gt = _inproj(x, norm_mix_g[0], w_in[0], gdn_conv_w[0], gdn_A_log[0],
                                             gdn_dt_bias[0], sconv_w[0])
    og = _gdn(q, k, v, zs, gbeta, gdn_norm_g[0])

    def flat(a):
        return a.reshape(t, a.shape[-1])

    bm = BM_MOE
    cap = pl.cdiv(t, bm) * bm
    h1, meta, tw, cnt, xpad = _mix(flat(og), flat(sb), flat(ga), flat(gt), flat(x), w_o_gdn[0], w_o_sconv[0],
                                   w_mix_out[0], norm_ffn_g[0], w_router[0], b_router[0], cap, bm)
    dest, block_e, n_used, x_block, n_slots = _routing_tables(meta, cnt, t, bm, cap)
    ypad = _moe(xpad, block_e, n_used, x_block, n_slots, w_gate_up[0], b_gate_up[0], w_down[0], b_down[0], bm)
    return _combine(ypad, dest, h1, tw, norm_final_g).reshape(bsz, seq, D_MODEL)
```

```python
import functools

import jax
import jax.numpy as jnp
from jax import lax
from jax.experimental import pallas as pl
from jax.experimental.pallas import tpu as pltpu

F32 = jnp.float32
BF16 = jnp.bfloat16
I32 = jnp.int32

D_MODEL = 1024
HEADS = 8
HEAD_DIM = 128
GDN_WIDTH = HEADS * HEAD_DIM
GDN_CONV = 4
CHUNK = 128
INV_BLOCK = 32
SC_CONV = 3
N_EXPERTS = 32
TOP_K = 4
D_FF = 1024
SWIGLU_LIMIT = 7.0
SWIGLU_ALPHA = 1.702
NORM_EPS = 1e-6

LANES = 128
SUBLANES = 8
VMEM_LIMIT = 56 * 1024 * 1024

TM_IN = 512
CT_IN = 256
TS_GDN = 512
NB_GDN = 2
TM_MIX = 512
ROW_BUFS = 3
MIX_SPLIT = 4
BM_MOE = 512
TM_COMB = 256
ISSUE_UNROLL = 8
DMA_PRIORITIES = 2
CAST_ROWS = 128


NEG_LOG2E = -1.4426950408889634


def _sigmoid(x):
    return 1.0 / (1.0 + jnp.exp2(x * NEG_LOG2E))


def _silu(x):
    return x * _sigmoid(x)


def _softplus(x):
    return jnp.maximum(x, 0.0) + jnp.log(1.0 + jnp.exp(-jnp.abs(x)))


def _rms_norm(x, g):
    return x * lax.rsqrt(jnp.mean(x * x, axis=-1, keepdims=True) + NORM_EPS) * g


def _dot(a, b):
    return jnp.dot(a, b, preferred_element_type=F32)


def _dot_nt(a, b):
    return lax.dot_general(a, b, (((1,), (1,)), ((), ())), preferred_element_type=F32)


def _dot_tn(a, b):
    return lax.dot_general(a, b, (((0,), (0,)), ((), ())), preferred_element_type=F32)


ROW_TILES = D_MODEL // LANES


def _store_row_tiles(ref, val):
    m = val.shape[0]
    for s in range(ROW_TILES):
        ref[pl.ds(s, m, stride=ROW_TILES), :] = val[:, s * LANES:(s + 1) * LANES]


def _load_row_tiles(ref, m, s):
    return ref[pl.ds(s, m, stride=ROW_TILES), :]


def _row_tile(ref, row):
    return ref.at[pl.ds(pl.multiple_of(row * ROW_TILES, ROW_TILES), ROW_TILES), :]


def _causal_conv(p, prev, w, stage_ref):
    taps = w.shape[0]
    tm = p.shape[0]
    stage_ref[0:SUBLANES, :] = prev
    stage_ref[SUBLANES:SUBLANES + tm, :] = p
    y = w[taps - 1:taps] * p
    for s in range(1, taps):
        y = y + w[taps - 1 - s:taps - s] * stage_ref[SUBLANES - s:SUBLANES - s + tm, :]
    return y


def _inproj_kernel(x_ref, ng_ref, wqkv_ref, wz_ref, wab_ref, wxs_ref, wgb_ref, wgc_ref, wga_ref, wgt_ref,
                   cw_ref, scw_ref, alog_ref, dtb_ref,
                   q_ref, k_ref, v_ref, zs_ref, gbeta_ref, sb_ref, ga_ref, gt_ref,
                   carry_qkv, carry_u, xn_scr, stage_ref):
    tm = x_ref.shape[0]

    @pl.when(pl.program_id(1) == 0)
    def _():
        carry_qkv[...] = jnp.zeros_like(carry_qkv)
        carry_u[...] = jnp.zeros_like(carry_u)

    x = x_ref[...]
    xn_scr[...] = _rms_norm(x, ng_ref[...]).astype(BF16)

    q_scale = HEAD_DIM ** -0.5
    outs = (q_ref, k_ref, v_ref)
    for ci in range(3 * GDN_WIDTH // CT_IN):
        c0 = ci * CT_IN
        p = _dot(xn_scr[...], wqkv_ref[:, c0:c0 + CT_IN])
        y = _causal_conv(p, carry_qkv[:, c0:c0 + CT_IN], cw_ref[:, c0:c0 + CT_IN], stage_ref.at[ci % 2])
        carry_qkv[:, c0:c0 + CT_IN] = p[tm - SUBLANES:tm]
        y = _silu(y)
        which = c0 // GDN_WIDTH
        o0 = c0 - which * GDN_WIDTH
        for hh in range(CT_IN // HEAD_DIM):
            yh = y[:, hh * HEAD_DIM:(hh + 1) * HEAD_DIM]
            if which < 2:
                yh = yh * lax.rsqrt(jnp.sum(yh * yh, axis=-1, keepdims=True) + NORM_EPS)
            if which == 0:
                yh = yh * q_scale
            outs[which][:, o0 + hh * HEAD_DIM:o0 + (hh + 1) * HEAD_DIM] = yh.astype(BF16)

    for ci in range(D_MODEL // CT_IN):
        c0 = ci * CT_IN
        zs_ref[:, c0:c0 + CT_IN] = _silu(_dot(xn_scr[...], wz_ref[:, c0:c0 + CT_IN])).astype(BF16)
        ga_ref[:, c0:c0 + CT_IN] = _sigmoid(_dot(xn_scr[...], wga_ref[:, c0:c0 + CT_IN])).astype(BF16)
        gt_ref[:, c0:c0 + CT_IN] = _sigmoid(_dot(xn_scr[...], wgt_ref[:, c0:c0 + CT_IN])).astype(BF16)

    for ci in range(D_MODEL // CT_IN):
        c0 = ci * CT_IN
        xs = _dot(xn_scr[...], wxs_ref[:, c0:c0 + CT_IN])
        gc = _dot(xn_scr[...], wgc_ref[:, c0:c0 + CT_IN])
        u = gc * xs
        y = _causal_conv(u, carry_u[:, c0:c0 + CT_IN], scw_ref[:, c0:c0 + CT_IN], stage_ref.at[ci % 2])
        carry_u[:, c0:c0 + CT_IN] = u[tm - SUBLANES:tm]
        gb = _dot(xn_scr[...], wgb_ref[:, c0:c0 + CT_IN])
        sb_ref[:, c0:c0 + CT_IN] = (gb * y).astype(BF16)

    ab = _dot(xn_scr[...], wab_ref[...])
    lane = lax.broadcasted_iota(I32, ab.shape, 1)
    g = -jnp.exp(alog_ref[...]) * _softplus(ab + dtb_ref[...])
    g_hi = g.astype(BF16).astype(F32)
    gbeta_ref[...] = jnp.where(lane < HEADS, g_hi, jnp.where(lane < 2 * HEADS, _sigmoid(ab), g - g_hi))


def _inproj(x, norm_g, w_in, conv_w, a_log, dt_bias, sconv_w):
    bsz, seq, _ = x.shape
    tm = min(TM_IN, seq)
    o = 0
    w = {}
    for name, width in (("qkv", 3 * GDN_WIDTH), ("z", GDN_WIDTH), ("a", HEADS), ("b", HEADS),
                        ("xs", D_MODEL), ("gb", D_MODEL), ("gc", D_MODEL), ("ga", D_MODEL), ("gt", D_MODEL)):
        w[name] = w_in[:, o:o + width]
        o += width
    wab = (jnp.zeros((D_MODEL, LANES), F32).at[:, :HEADS].set(w["a"]).at[:, HEADS:2 * HEADS].set(w["b"])
           .at[:, 2 * HEADS:3 * HEADS].set(w["a"]))
    alog = jnp.zeros((1, LANES), F32).at[0, :HEADS].set(a_log).at[0, 2 * HEADS:3 * HEADS].set(a_log)
    dtb = jnp.zeros((1, LANES), F32).at[0, :HEADS].set(dt_bias).at[0, 2 * HEADS:3 * HEADS].set(dt_bias)
    weights = [w["qkv"], w["z"], wab, w["xs"], w["gb"], w["gc"], w["ga"], w["gt"]]
    weights = [a.astype(BF16) for a in weights]

    def const(shape):
        return pl.BlockSpec(shape, lambda b, s: (0,) * len(shape), pipeline_mode=pl.Buffered(1))

    def tok(width):
        return pl.BlockSpec((None, tm, width), lambda b, s: (b, s, 0))

    out_bf = jax.ShapeDtypeStruct((bsz, seq, D_MODEL), BF16)
    return pl.pallas_call(
        _inproj_kernel,
        grid=(bsz, seq // tm),
        in_specs=[tok(D_MODEL), const((1, D_MODEL))] + [const(a.shape) for a in weights]
        + [const((GDN_CONV, 3 * GDN_WIDTH)), const((SC_CONV, D_MODEL)), const((1, LANES)), const((1, LANES))],
        out_specs=[tok(D_MODEL)] * 4 + [tok(LANES)] + [tok(D_MODEL)] * 3,
        out_shape=[out_bf] * 4 + [jax.ShapeDtypeStruct((bsz, seq, LANES), F32)] + [out_bf] * 3,
        scratch_shapes=[pltpu.VMEM((SUBLANES, 3 * GDN_WIDTH), F32), pltpu.VMEM((SUBLANES, D_MODEL), F32),
                        pltpu.VMEM((tm, D_MODEL), BF16), pltpu.VMEM((2, SUBLANES + tm, CT_IN), F32)],
        compiler_params=pltpu.CompilerParams(dimension_semantics=("arbitrary", "arbitrary"),
                                             vmem_limit_bytes=VMEM_LIMIT),
        name="inproj",
    )(x, norm_g.reshape(1, D_MODEL), *weights, conv_w, sconv_w, alog, dtb)


def _gdn_kernel(q_ref, k_ref, v_ref, zs_ref, gbeta_ref, ng_ref, o_ref, state_ref):
    nb, ts = q_ref.shape[0], q_ref.shape[1]

    @pl.when(pl.program_id(1) == 0)
    def _():
        state_ref[...] = jnp.zeros_like(state_ref)

    ri = lax.broadcasted_iota(I32, (CHUNK, CHUNK), 0)
    ci = lax.broadcasted_iota(I32, (CHUNK, CHUNK), 1)
    causal = ri >= ci
    strict = ri > ci
    ltri = causal.astype(BF16)
    eye = (ri == ci).astype(F32)
    sh = INV_BLOCK.bit_length() - 1
    diag_blk = (ri >> sh) == (ci >> sh)
    off_blks = []
    while (1 << sh) < CHUNK:
        off_blks.append(((ri >> (sh + 1)) == (ci >> (sh + 1))) & ((ri >> sh) != (ci >> sh)) & strict)
        sh += 1
    ng = ng_ref[...]
    hs = range(nb * HEADS)
    bi = [u // HEADS for u in hs]
    hi = [u % HEADS for u in hs]

    def chunk(c, carry):
        rows = pl.ds(pl.multiple_of(c * CHUNK, CHUNK), CHUNK)
        cols = [slice(hi[h] * HEAD_DIM, (hi[h] + 1) * HEAD_DIM) for h in hs]
        gbc = [gbeta_ref[b, rows, :] for b in range(nb)]
        gcum = [_dot(ltri, gbc[b].astype(BF16)) for b in range(nb)]
        gcum_t = [gcum[b].T for b in range(nb)]
        gc = [gcum[bi[h]][:, hi[h]:hi[h] + 1] + gcum[bi[h]][:, 2 * HEADS + hi[h]:2 * HEADS + hi[h] + 1]
              for h in hs]
        gr = [gcum_t[bi[h]][hi[h]:hi[h] + 1, :] + gcum_t[bi[h]][2 * HEADS + hi[h]:2 * HEADS + hi[h] + 1, :]
              for h in hs]
        beta = [gbc[bi[h]][:, HEADS + hi[h]:HEADS + hi[h] + 1] for h in hs]
        g_last = [gc[h][CHUNK - 1:CHUNK, :] for h in hs]
        decay = [jnp.where(causal, jnp.exp(jnp.where(causal, gc[h] - gr[h], 0.0)), 0.0) for h in hs]
        eg = [jnp.exp(gc[h]) for h in hs]

        kf = [k_ref[bi[h], rows, cols[h]].astype(F32) for h in hs]
        kb = [kf[h] * beta[h] for h in hs]
        kt = [kf[h].T for h in hs]
        ktb = [kt[h].astype(BF16) for h in hs]
        lhs = [jnp.concatenate([kb[h].astype(BF16), q_ref[bi[h], rows, cols[h]]], axis=0) for h in hs]
        aq = [_dot(lhs[h], ktb[h]) for h in hs]
        aqk = [(aq[h][CHUNK:] * decay[h]).astype(BF16) for h in hs]
        lm = [jnp.where(strict, aq[h][:CHUNK] * decay[h], 0.0) for h in hs]
        m = [jnp.where(diag_blk, -lm[h], 0.0) for h in hs]
        tinv = [eye + m[h] for h in hs]
        mb = [m[h].astype(BF16) for h in hs]
        m = [_dot(mb[h], mb[h]) for h in hs]
        levels = INV_BLOCK.bit_length() - 1
        for j in range(1, levels):
            mb = [m[h].astype(BF16) for h in hs]
            if j < levels - 1:
                st = [_dot(jnp.concatenate([tinv[h].astype(BF16), mb[h]], axis=0), mb[h]) for h in hs]
                tinv = [tinv[h] + st[h][:CHUNK] for h in hs]
                m = [st[h][CHUNK:] for h in hs]
            else:
                tinv = [tinv[h] + _dot(tinv[h].astype(BF16), mb[h]) for h in hs]
        for off in off_blks:
            tb = [tinv[h].astype(BF16) for h in hs]
            ct = [_dot(jnp.where(off, lm[h], 0.0).astype(BF16), tb[h]) for h in hs]
            tinv = [tinv[h] - _dot(tb[h], ct[h].astype(BF16)) for h in hs]
        rhs = [jnp.concatenate([(v_ref[bi[h], rows, cols[h]].astype(F32) * beta[h]).astype(BF16),
                                (kb[h] * eg[h]).astype(BF16)], axis=1) for h in hs]
        uw = [_dot(tinv[h].astype(BF16), rhs[h]) for h in hs]

        sts = [state_ref[h] for h in hs]
        stb = [sts[h].astype(BF16) for h in hs]
        wq = [jnp.concatenate([uw[h][:, HEAD_DIM:].astype(BF16),
                               (q_ref[bi[h], rows, cols[h]].astype(F32) * eg[h]).astype(BF16)], axis=0) for h in hs]
        ws = [_dot(wq[h], stb[h]) for h in hs]
        vnb = [(uw[h][:, :HEAD_DIM] - ws[h][:CHUNK]).astype(BF16) for h in hs]
        kdt = [(kt[h] * jnp.exp(g_last[h] - gr[h])).astype(BF16) for h in hs]
        o = [ws[h][CHUNK:] + _dot(aqk[h], vnb[h]) for h in hs]
        for h in hs:
            state_ref[h] = sts[h] * jnp.exp(g_last[h]) + _dot(kdt[h], vnb[h])
        for h in hs:
            oh = _rms_norm(o[h], ng) * zs_ref[bi[h], rows, cols[h]].astype(F32)
            o_ref[bi[h], rows, cols[h]] = oh.astype(BF16)
        return carry

    lax.fori_loop(0, ts // CHUNK, chunk, 0)


def _gdn(q, k, v, zs, gbeta, norm_g):
    bsz, seq, _ = q.shape
    ts = min(TS_GDN, seq)
    nb = NB_GDN if bsz % NB_GDN == 0 else 1

    def tok(width):
        return pl.BlockSpec((nb, ts, width), lambda b, s: (b, s, 0))

    return pl.pallas_call(
        _gdn_kernel,
        grid=(bsz // nb, seq // ts),
        in_specs=[tok(GDN_WIDTH)] * 4 + [tok(LANES), pl.BlockSpec((1, HEAD_DIM), lambda b, s: (0, 0))],
        out_specs=tok(GDN_WIDTH),
        out_shape=jax.ShapeDtypeStruct((bsz, seq, GDN_WIDTH), BF16),
        scratch_shapes=[pltpu.VMEM((nb * HEADS, HEAD_DIM, HEAD_DIM), F32)],
        compiler_params=pltpu.CompilerParams(dimension_semantics=("arbitrary", "arbitrary"),
                                             vmem_limit_bytes=VMEM_LIMIT),
        name="gdn",
    )(q, k, v, zs, gbeta, norm_g.reshape(1, HEAD_DIM))


def _mix_kernel(og_ref, sb_ref, ga_ref, gt_ref, x_ref, wog_ref, wos_ref, wmix_ref, gffn_ref, wr_ref, br_ref,
                h1_ref, meta_ref, tw_ref, cnt_ref, xpad_ref,
                carry_ref, xt_ref, dvm_ref, dsm_ref, cvm_ref, csm_ref, zero_ref, sem_rows, sem_s, sem_z,
                *, nt, cap, bm):
    tm = x_ref.shape[0]
    i = pl.program_id(0)
    slot = i % ROW_BUFS

    def wait_rows(s):
        for _ in range(TOP_K):
            pltpu.make_async_copy(xt_ref.at[s], xpad_ref.at[pl.ds(0, tm * ROW_TILES), :], sem_rows.at[s]).wait()

    def slots_copy(par):
        return pltpu.make_async_copy(dvm_ref.at[par], dsm_ref.at[par], sem_s.at[par])

    def issue_rows(s, par):
        def issue(g, carry):
            for u in range(ISSUE_UNROLL):
                tok = g * (ISSUE_UNROLL // TOP_K) + u // TOP_K
                pltpu.make_async_copy(_row_tile(xt_ref.at[s], tok),
                                      _row_tile(xpad_ref, dsm_ref[par, u % TOP_K, tok]),
                                      sem_rows.at[s]).start(priority=u % DMA_PRIORITIES)
            return carry

        lax.fori_loop(0, tm * TOP_K // ISSUE_UNROLL, issue, 0)

    @pl.when(i == 0)
    def _():
        carry_ref[...] = jnp.zeros_like(carry_ref)

    @pl.when(i >= ROW_BUFS)
    def _():
        wait_rows(slot)

    sub = tm // MIX_SPLIT
    rs = [slice(r * sub, (r + 1) * sub) for r in range(MIX_SPLIT)]
    ya = [_dot(og_ref[r, :], wog_ref[...]) for r in rs]
    yb = [_dot(sb_ref[r, :], wos_ref[...]) for r in rs]
    merged = [(ga_ref[r, :].astype(F32) * a + gt_ref[r, :].astype(F32) * b).astype(BF16)
              for r, a, b in zip(rs, ya, yb)]
    h1 = [x_ref[r, :] + _dot(mg, wmix_ref[...]) for r, mg in zip(rs, merged)]
    xn2 = [_rms_norm(h, gffn_ref[...]) for h in h1]
    x_hi = [v.astype(BF16) for v in xn2]
    x_lo = [(v - hi.astype(F32)).astype(BF16) for v, hi in zip(xn2, x_hi)]
    parts = [_dot(hi, wr_ref[...]) + _dot(lo, wr_ref[...]) for hi, lo in zip(x_hi, x_lo)]
    for r, h in zip(rs, h1):
        h1_ref[r, :] = h
    _store_row_tiles(xt_ref.at[slot], jnp.concatenate(xn2, axis=0))
    logits =jnp.concatenate([p[:, :LANES] + p[:, LANES:] for p in parts], axis=0) + br_ref[...]

    lane = lax.broadcasted_iota(I32, (tm, LANES), 1)
    lane_f = lane.astype(F32)
    neg_inf = jnp.float32(-jnp.inf)
    work = jnp.where(lane < N_EXPERTS, logits, neg_inf)
    hits, vals = [], []
    for _ in range(TOP_K):
        m = jnp.max(work, axis=-1, keepdims=True)
        first = jnp.min(jnp.where(work == m, lane_f, float(LANES)), axis=-1, keepdims=True)
        hit = lane_f == first
        work = jnp.where(hit, neg_inf, work)
        hits.append(hit)
        vals.append((m, first))
    exps = [jnp.exp(m - vals[0][0]) for m, _ in vals]
    denom = exps[0] + exps[1] + exps[2] + exps[3]

    sel = jnp.zeros((tm, LANES), F32)
    for hit in hits:
        sel = sel + hit.astype(F32)
    ri = lax.broadcasted_iota(I32, (tm, tm), 0)
    ci = lax.broadcasted_iota(I32, (tm, tm), 1)
    before = (ri > ci).astype(BF16)
    rank_mat = _dot(before, sel.astype(BF16)) + carry_ref[...]
    carry_ref[...] = carry_ref[...] + jnp.sum(sel, axis=0, keepdims=True)
    cnt_ref[...] = carry_ref[...]

    meta = jnp.zeros((tm, LANES), F32)
    rows = jnp.zeros((tm, LANES), F32)
    tw_out = jnp.zeros((tm, LANES), F32)
    for kk in range(TOP_K):
        rk = jnp.sum(jnp.where(hits[kk], rank_mat, 0.0), axis=-1, keepdims=True)
        meta = jnp.where(lane == kk, vals[kk][1], meta)
        meta = jnp.where(lane == TOP_K + kk, rk, meta)
        rows = jnp.where(lane == kk, vals[kk][1] * float(cap) + rk, rows)
        tw_out = jnp.where(lane == kk, exps[kk] / denom, tw_out)
    meta_ref[0] = meta.T[0:2 * TOP_K, :].astype(I32)
    tw_ref[...] = tw_out

    par = i % 2
    dvm_ref[par] = rows.T[0:SUBLANES, :].astype(I32)
    slots_copy(par).start()

    @pl.when(i >= 1)
    def _():
        slots_copy(1 - par).wait()
        issue_rows((i + ROW_BUFS - 1) % ROW_BUFS, 1 - par)

    @pl.when(i == nt - 1)
    def _():
        slots_copy(par).wait()
        issue_rows(slot, par)
        for back in range(min(ROW_BUFS, nt)):
            wait_rows((i + ROW_BUFS - back) % ROW_BUFS)
        cvm_ref[...] = jnp.broadcast_to(carry_ref[...], cvm_ref.shape).astype(I32)
        counts_to_smem = pltpu.make_async_copy(cvm_ref, csm_ref, sem_z)
        counts_to_smem.start()
        counts_to_smem.wait()
        zero_ref[...] = jnp.zeros_like(zero_ref)
        for e in range(N_EXPERTS):
            count = csm_ref[0, e]
            padded = (count + (bm - 1)) // bm * bm

            def zero_row(r, carry):
                pltpu.make_async_copy(zero_ref, _row_tile(xpad_ref, e * cap + r), sem_z).start()
                return carry

            def zero_wait(r, carry):
                pltpu.make_async_copy(zero_ref, _row_tile(xpad_ref, 0), sem_z).wait()
                return carry

            lax.fori_loop(count, padded, zero_row, 0)
            lax.fori_loop(count, padded, zero_wait, 0)


def _mix(og, sb, ga, gt, x, w_o_gdn, w_o_sconv, w_mix_out, norm_ffn_g, w_router, b_router, cap, bm):
    t = x.shape[0]
    tm = min(TM_MIX, t)
    nt = t // tm
    assert N_EXPERTS * cap < 2 ** 24, "row indices are formed in f32"
    wr = jnp.zeros((D_MODEL, LANES), F32).at[:, :N_EXPERTS].set(w_router)
    wr_hi = wr.astype(BF16)
    wr = jnp.concatenate([wr_hi, (wr - wr_hi.astype(F32)).astype(BF16)], axis=1)
    br = jnp.zeros((1, LANES), F32).at[0, :N_EXPERTS].set(b_router)

    def const(shape):
        return pl.BlockSpec(shape, lambda i: (0,) * len(shape), pipeline_mode=pl.Buffered(1))

    def tok(rows, width):
        return pl.BlockSpec((rows, width), lambda i: (i, 0))

    return pl.pallas_call(
        functools.partial(_mix_kernel, nt=nt, cap=cap, bm=bm),
        grid=(nt,),
        in_specs=[tok(tm, D_MODEL)] * 5 + [const((D_MODEL, D_MODEL))] * 3
        + [const((1, D_MODEL)), const((D_MODEL, 2 * LANES)), const((1, LANES))],
        out_specs=[tok(tm, D_MODEL), pl.BlockSpec((1, 2 * TOP_K, tm), lambda i: (i, 0, 0)), tok(tm, LANES),
                   pl.BlockSpec((1, LANES), lambda i: (0, 0)), pl.BlockSpec(memory_space=pl.ANY)],
        out_shape=[jax.ShapeDtypeStruct((t, D_MODEL), F32), jax.ShapeDtypeStruct((nt, 2 * TOP_K, tm), I32),
                   jax.ShapeDtypeStruct((t, LANES), F32), jax.ShapeDtypeStruct((1, LANES), F32),
                   jax.ShapeDtypeStruct((N_EXPERTS * cap * ROW_TILES, LANES), F32)],
        scratch_shapes=[pltpu.VMEM((1, LANES), F32), pltpu.VMEM((ROW_BUFS, tm * ROW_TILES, LANES), F32),
                        pltpu.VMEM((2, SUBLANES, tm), I32), pltpu.SMEM((2, SUBLANES, tm), I32),
                        pltpu.VMEM((SUBLANES, LANES), I32), pltpu.SMEM((SUBLANES, LANES), I32),
                        pltpu.VMEM((ROW_TILES, LANES), F32),
                        pltpu.SemaphoreType.DMA((ROW_BUFS,)), pltpu.SemaphoreType.DMA((2,)),
                        pltpu.SemaphoreType.DMA(())],
        compiler_params=pltpu.CompilerParams(dimension_semantics=("arbitrary",), vmem_limit_bytes=VMEM_LIMIT,
                                             has_side_effects=True, disable_bounds_checks=True),
        name="mix_router",
    )(og, sb, ga, gt, x, w_o_gdn.astype(BF16), w_o_sconv.astype(BF16), w_mix_out.astype(BF16),
      norm_ffn_g.reshape(1, D_MODEL), wr, br)


def _moe_kernel(be_ref, nu_ref, xr_ref, x_ref, wgu_ref, bgu_ref, wd_ref, bd_ref, y_ref, wgu_bf, wd_bf):
    bm = x_ref.shape[0] // ROW_TILES
    i = pl.program_id(0)

    @pl.when(i < nu_ref[0])
    def _():
        @pl.when((i == 0) | (be_ref[i] != be_ref[jnp.maximum(i - 1, 0)]))
        def _():
            def cast_rows(r, carry):
                rows = pl.ds(pl.multiple_of(r * CAST_ROWS, CAST_ROWS), CAST_ROWS)
                wgu_bf[rows, :] = wgu_ref[rows, :].astype(BF16)
                wd_bf[rows, :] = wd_ref[rows, :].astype(BF16)
                return carry

            lax.fori_loop(0, D_MODEL // CAST_ROWS, cast_rows, 0)

        xb = jnp.concatenate([_load_row_tiles(x_ref, bm, s).astype(BF16) for s in range(ROW_TILES)], axis=1)
        gu = _dot(xb, wgu_bf[...]) + bgu_ref[...]
        gate = jnp.minimum(gu[:, :D_FF], SWIGLU_LIMIT)
        up = jnp.clip(gu[:, D_FF:], -SWIGLU_LIMIT, SWIGLU_LIMIT)
        hid = (up + 1.0) * gate * _sigmoid(SWIGLU_ALPHA * gate)
        _store_row_tiles(y_ref, _dot(hid.astype(BF16), wd_bf[...]) + bd_ref[...])


def _moe(xpad, block_e, n_used, x_block, n_slots, w_gate_up, b_gate_up, w_down, b_down, bm):
    nb = n_slots // bm

    def rows(i, be, nu, xr):
        return (jnp.minimum(i, nu[0] - 1), 0)

    def x_rows(i, be, nu, xr):
        return (xr[i], 0)

    def expert(i, be, nu, xr):
        return (be[i], 0, 0)

    grid_spec = pltpu.PrefetchScalarGridSpec(
        num_scalar_prefetch=3,
        grid=(nb,),
        in_specs=[pl.BlockSpec((bm * ROW_TILES, LANES), x_rows),
                  pl.BlockSpec((None, D_MODEL, 2 * D_FF), expert),
                  pl.BlockSpec((None, 1, 2 * D_FF), expert),
                  pl.BlockSpec((None, D_FF, D_MODEL), expert),
                  pl.BlockSpec((None, 1, D_MODEL), expert)],
        out_specs=pl.BlockSpec((bm * ROW_TILES, LANES), rows),
        scratch_shapes=[pltpu.VMEM((D_MODEL, 2 * D_FF), BF16), pltpu.VMEM((D_FF, D_MODEL), BF16)],
    )
    return pl.pallas_call(
        _moe_kernel,
        grid_spec=grid_spec,
        out_shape=jax.ShapeDtypeStruct((n_slots * ROW_TILES, LANES), F32),
        compiler_params=pltpu.CompilerParams(dimension_semantics=("arbitrary",), vmem_limit_bytes=VMEM_LIMIT),
        name="moe_mlp",
    )(block_e, n_used, x_block, xpad, w_gate_up, b_gate_up.reshape(N_EXPERTS, 1, 2 * D_FF),
      w_down, b_down.reshape(N_EXPERTS, 1, D_MODEL))


def _combine_kernel(dest_ref, dest_next_ref, h1_ref, tw_ref, gf_ref, ypad_ref, out_ref, buf_ref, sem):
    tm = h1_ref.shape[0]
    i = pl.program_id(0)
    cur = i % 2

    def gather(d_ref, parity):
        def issue(g, carry):
            for u in range(ISSUE_UNROLL):
                tok = g * (ISSUE_UNROLL // TOP_K) + u // TOP_K
                pltpu.make_async_copy(_row_tile(ypad_ref, d_ref[0, 0, g * ISSUE_UNROLL + u]),
                                      _row_tile(buf_ref.at[parity * TOP_K + u % TOP_K], tok),
                                      sem.at[parity]).start(priority=u % DMA_PRIORITIES)
            return carry

        lax.fori_loop(0, tm * TOP_K // ISSUE_UNROLL, issue, 0)

    @pl.when(i == 0)
    def _():
        gather(dest_ref, 0)

    @pl.when(i + 1 < pl.num_programs(0))
    def _():
        gather(dest_next_ref, 1 - cur)

    for kk in range(TOP_K):
        pltpu.make_async_copy(ypad_ref.at[pl.ds(0, tm * ROW_TILES), :], buf_ref.at[cur * TOP_K + kk],
                              sem.at[cur]).wait()

    tw = tw_ref[...]
    accs = []
    ssq = jnp.zeros((tm, 1), F32)
    for s in range(ROW_TILES):
        acc = h1_ref[:, s * LANES:(s + 1) * LANES]
        for kk in range(TOP_K):
            acc = acc + tw[:, kk:kk + 1] * _load_row_tiles(buf_ref.at[cur * TOP_K + kk], tm, s)
        ssq = ssq + jnp.sum(acc * acc, axis=-1, keepdims=True)
        accs.append(acc)
    inv = lax.rsqrt(ssq * (1.0 / D_MODEL) + NORM_EPS)
    for s in range(ROW_TILES):
        out_ref[:, s * LANES:(s + 1) * LANES] = accs[s] * inv * gf_ref[:, s * LANES:(s + 1) * LANES]


def _combine(ypad, dest, h1, tw, norm_final_g):
    t = h1.shape[0]
    tm = min(TM_COMB, t)
    nt = t // tm
    dest3 = dest.reshape(nt, 1, tm * TOP_K)
    return pl.pallas_call(
        _combine_kernel,
        grid=(nt,),
        in_specs=[pl.BlockSpec((1, 1, tm * TOP_K), lambda i: (i, 0, 0), memory_space=pltpu.SMEM),
                  pl.BlockSpec((1, 1, tm * TOP_K), lambda i: (jnp.minimum(i + 1, nt - 1), 0, 0),
                               memory_space=pltpu.SMEM),
                  pl.BlockSpec((tm, D_MODEL), lambda i: (i, 0)),
                  pl.BlockSpec((tm, LANES), lambda i: (i, 0)),
                  pl.BlockSpec((1, D_MODEL), lambda i: (0, 0)),
                  pl.BlockSpec(memory_space=pl.ANY)],
        out_specs=pl.BlockSpec((tm, D_MODEL), lambda i: (i, 0)),
        out_shape=jax.ShapeDtypeStruct((t, D_MODEL), F32),
        scratch_shapes=[pltpu.VMEM((2 * TOP_K, tm * ROW_TILES, LANES), F32), pltpu.SemaphoreType.DMA((2,))],
        compiler_params=pltpu.CompilerParams(dimension_semantics=("arbitrary",), vmem_limit_bytes=VMEM_LIMIT,
                                             disable_bounds_checks=True),
        name="combine",
    )(dest3, dest3, h1, tw, norm_final_g.reshape(1, D_MODEL), ypad)


def _routing_tables(meta, cnt, t, bm, cap):
    counts = cnt[0, :N_EXPERTS].astype(I32)
    pcounts = ((counts + bm - 1) // bm) * bm
    pend = jnp.cumsum(pcounts)
    pstart = (pend - pcounts).astype(I32)
    idx, rank = meta[:, :TOP_K, :], meta[:, TOP_K:, :]
    onehot = idx[..., None] == jnp.arange(N_EXPERTS, dtype=I32)
    dest = jnp.sum(jnp.where(onehot, pstart, 0), axis=-1).astype(I32) + rank
    dest = dest.transpose(0, 2, 1).reshape(-1)
    n_slots = ((t * TOP_K + N_EXPERTS * (bm - 1) + bm - 1) // bm) * bm
    nb = n_slots // bm
    n_used = pend[-1] // bm
    starts = jnp.arange(nb, dtype=I32) * bm
    block_e = jnp.minimum(jnp.sum((pend[None, :] <= starts[:, None]).astype(I32), axis=1), N_EXPERTS - 1)
    x_block = block_e * (cap // bm) + (starts - pstart[block_e]) // bm
    last = jnp.maximum(n_used - 1, 0)
    used = jnp.arange(nb) < n_used
    block_e = jnp.where(used, block_e, block_e[last]).astype(I32)
    x_block = jnp.where(used, x_block, x_block[last]).astype(I32)
    return dest, block_e, n_used.reshape(1).astype(I32), x_block, n_slots


def kernel(x, norm_mix_g, w_in, gdn_conv_w, gdn_A_log, gdn_dt_bias, gdn_norm_g, w_o_gdn, sconv_w, w_o_sconv,
           w_mix_out, norm_ffn_g, w_router, b_router, w_gate_up, b_gate_up, w_down, b_down, norm_final_g):
    bsz, seq, _ = x.shape
    t = bsz * seq
    assert norm_mix_g.shape[0] == 1, "single-layer block only"
    q, k, v, zs, gbeta, sb, ga, gt = _inproj(x, norm_mix_g[0], w_in[0], gdn_conv_w[0], gdn_A_log[0],
                                             gdn_dt_bias[0], sconv_w[0])
    og = _gdn(q, k, v, zs, gbeta, gdn_norm_g[0])

    def flat(a):
        return a.reshape(t, a.shape[-1])

    bm = BM_MOE
    cap = pl.cdiv(t, bm) * bm
    h1, meta, tw, cnt, xpad = _mix(flat(og), flat(sb), flat(ga), flat(gt), flat(x), w_o_gdn[0], w_o_sconv[0],
                                   w_mix_out[0], norm_ffn_g[0], w_router[0], b_router[0], cap, bm)
    dest, block_e, n_used, x_block, n_slots = _routing_tables(meta, cnt, t, bm, cap)
    ypad = _moe(xpad, block_e, n_used, x_block, n_slots, w_gate_up[0], b_gate_up[0], w_down[0], b_down[0], bm)
    return _combine(ypad, dest, h1, tw, norm_final_g).reshape(bsz, seq, D_MODEL)
```

```python
import functools

import jax
import jax.numpy as jnp
from jax import lax
from jax.experimental import pallas as pl
from jax.experimental.pallas import tpu as pltpu

F32 = jnp.float32
BF16 = jnp.bfloat16
I32 = jnp.int32

D_MODEL = 1024
HEADS = 8
HEAD_DIM = 128
GDN_WIDTH = HEADS * HEAD_DIM
GDN_CONV = 4
CHUNK = 128
INV_BLOCK = 32
SC_CONV = 3
N_EXPERTS = 32
TOP_K = 4
D_FF = 1024
SWIGLU_LIMIT = 7.0
SWIGLU_ALPHA = 1.702
NORM_EPS = 1e-6

LANES = 128
SUBLANES = 8
VMEM_LIMIT = 56 * 1024 * 1024

TM_IN = 512
CT_IN = 256
TS_GDN = 512
NB_GDN = 2
TM_MIX = 512
ROW_BUFS = 3
MIX_SPLIT = 4
BM_MOE = 512
TM_COMB = 256
COMB_ROWS = 64
ISSUE_UNROLL = 8
CAST_ROWS = 128


NEG_LOG2E = -1.4426950408889634


def _sigmoid(x):
    return 1.0 / (1.0 + jnp.exp2(x * NEG_LOG2E))


def _silu(x):
    return x * _sigmoid(x)


def _softplus(x):
    return jnp.maximum(x, 0.0) + jnp.log(1.0 + jnp.exp(-jnp.abs(x)))


def _rms_norm(x, g):
    return x * lax.rsqrt(jnp.mean(x * x, axis=-1, keepdims=True) + NORM_EPS) * g


def _dot(a, b):
    return jnp.dot(a, b, preferred_element_type=F32)


def _dot_nt(a, b):
    return lax.dot_general(a, b, (((1,), (1,)), ((), ())), preferred_element_type=F32)


def _dot_tn(a, b):
    return lax.dot_general(a, b, (((0,), (0,)), ((), ())), preferred_element_type=F32)


ROW_TILES = D_MODEL // LANES


def _store_row_tiles(ref, val):
    m = val.shape[0]
    for s in range(ROW_TILES):
        ref[pl.ds(s, m, stride=ROW_TILES), :] = val[:, s * LANES:(s + 1) * LANES]


def _load_row_tiles(ref, m, s):
    return ref[pl.ds(s, m, stride=ROW_TILES), :]


def _row_tile(ref, row):
    return ref.at[pl.ds(pl.multiple_of(row * ROW_TILES, ROW_TILES), ROW_TILES), :]


def _causal_conv(p, prev, w, stage_ref):
    taps = w.shape[0]
    tm = p.shape[0]
    stage_ref[0:SUBLANES, :] = prev
    stage_ref[SUBLANES:SUBLANES + tm, :] = p
    y = w[taps - 1:taps] * p
    for s in range(1, taps):
        y = y + w[taps - 1 - s:taps - s] * stage_ref[SUBLANES - s:SUBLANES - s + tm, :]
    return y


def _inproj_kernel(x_ref, ng_ref, wqkv_ref, wz_ref, wab_ref, wxs_ref, wgb_ref, wgc_ref, wga_ref, wgt_ref,
                   cw_ref, scw_ref, alog_ref, dtb_ref,
                   q_ref, k_ref, v_ref, zs_ref, gbeta_ref, sb_ref, ga_ref, gt_ref,
                   carry_qkv, carry_u, xn_scr, stage_ref):
    tm = x_ref.shape[0]

    @pl.when(pl.program_id(1) == 0)
    def _():
        carry_qkv[...] = jnp.zeros_like(carry_qkv)
        carry_u[...] = jnp.zeros_like(carry_u)

    x = x_ref[...]
    xn_scr[...] = _rms_norm(x, ng_ref[...]).astype(BF16)

    q_scale = HEAD_DIM ** -0.5
    outs = (q_ref, k_ref, v_ref)
    for ci in range(3 * GDN_WIDTH // CT_IN):
        c0 = ci * CT_IN
        p = _dot(xn_scr[...], wqkv_ref[:, c0:c0 + CT_IN])
        y = _causal_conv(p, carry_qkv[:, c0:c0 + CT_IN], cw_ref[:, c0:c0 + CT_IN], stage_ref.at[ci % 2])
        carry_qkv[:, c0:c0 + CT_IN] = p[tm - SUBLANES:tm]
        y = _silu(y)
        which = c0 // GDN_WIDTH
        o0 = c0 - which * GDN_WIDTH
        for hh in range(CT_IN // HEAD_DIM):
            yh = y[:, hh * HEAD_DIM:(hh + 1) * HEAD_DIM]
            if which < 2:
                yh = yh * lax.rsqrt(jnp.sum(yh * yh, axis=-1, keepdims=True) + NORM_EPS)
            if which == 0:
                yh = yh * q_scale
            outs[which][:, o0 + hh * HEAD_DIM:o0 + (hh + 1) * HEAD_DIM] = yh.astype(BF16)

    for ci in range(D_MODEL // CT_IN):
        c0 = ci * CT_IN
        zs_ref[:, c0:c0 + CT_IN] = _silu(_dot(xn_scr[...], wz_ref[:, c0:c0 + CT_IN])).astype(BF16)
        ga_ref[:, c0:c0 + CT_IN] = _sigmoid(_dot(xn_scr[...], wga_ref[:, c0:c0 + CT_IN])).astype(BF16)
        gt_ref[:, c0:c0 + CT_IN] = _sigmoid(_dot(xn_scr[...], wgt_ref[:, c0:c0 + CT_IN])).astype(BF16)

    for ci in range(D_MODEL // CT_IN):
        c0 = ci * CT_IN
        xs = _dot(xn_scr[...], wxs_ref[:, c0:c0 + CT_IN])
        gc = _dot(xn_scr[...], wgc_ref[:, c0:c0 + CT_IN])
        u = gc * xs
        y = _causal_conv(u, carry_u[:, c0:c0 + CT_IN], scw_ref[:, c0:c0 + CT_IN], stage_ref.at[ci % 2])
        carry_u[:, c0:c0 + CT_IN] = u[tm - SUBLANES:tm]
        gb = _dot(xn_scr[...], wgb_ref[:, c0:c0 + CT_IN])
        sb_ref[:, c0:c0 + CT_IN] = (gb * y).astype(BF16)

    ab = _dot(xn_scr[...], wab_ref[...])
    lane = lax.broadcasted_iota(I32, ab.shape, 1)
    g = -jnp.exp(alog_ref[...]) * _softplus(ab + dtb_ref[...])
    g_hi = g.astype(BF16).astype(F32)
    gbeta_ref[...] = jnp.where(lane < HEADS, g_hi, jnp.where(lane < 2 * HEADS, _sigmoid(ab), g - g_hi))


def _inproj(x, norm_g, w_in, conv_w, a_log, dt_bias, sconv_w):
    bsz, seq, _ = x.shape
    tm = min(TM_IN, seq)
    o = 0
    w = {}
    for name, width in (("qkv", 3 * GDN_WIDTH), ("z", GDN_WIDTH), ("a", HEADS), ("b", HEADS),
                        ("xs", D_MODEL), ("gb", D_MODEL), ("gc", D_MODEL), ("ga", D_MODEL), ("gt", D_MODEL)):
        w[name] = w_in[:, o:o + width]
        o += width
    wab = (jnp.zeros((D_MODEL, LANES), F32).at[:, :HEADS].set(w["a"]).at[:, HEADS:2 * HEADS].set(w["b"])
           .at[:, 2 * HEADS:3 * HEADS].set(w["a"]))
    alog = jnp.zeros((1, LANES), F32).at[0, :HEADS].set(a_log).at[0, 2 * HEADS:3 * HEADS].set(a_log)
    dtb = jnp.zeros((1, LANES), F32).at[0, :HEADS].set(dt_bias).at[0, 2 * HEADS:3 * HEADS].set(dt_bias)
    weights = [w["qkv"], w["z"], wab, w["xs"], w["gb"], w["gc"], w["ga"], w["gt"]]
    weights = [a.astype(BF16) for a in weights]

    def const(shape):
        return pl.BlockSpec(shape, lambda b, s: (0,) * len(shape), pipeline_mode=pl.Buffered(1))

    def tok(width):
        return pl.BlockSpec((None, tm, width), lambda b, s: (b, s, 0))

    out_bf = jax.ShapeDtypeStruct((bsz, seq, D_MODEL), BF16)
    return pl.pallas_call(
        _inproj_kernel,
        grid=(bsz, seq // tm),
        in_specs=[tok(D_MODEL), const((1, D_MODEL))] + [const(a.shape) for a in weights]
        + [const((GDN_CONV, 3 * GDN_WIDTH)), const((SC_CONV, D_MODEL)), const((1, LANES)), const((1, LANES))],
        out_specs=[tok(D_MODEL)] * 4 + [tok(LANES)] + [tok(D_MODEL)] * 3,
        out_shape=[out_bf] * 4 + [jax.ShapeDtypeStruct((bsz, seq, LANES), F32)] + [out_bf] * 3,
        scratch_shapes=[pltpu.VMEM((SUBLANES, 3 * GDN_WIDTH), F32), pltpu.VMEM((SUBLANES, D_MODEL), F32),
                        pltpu.VMEM((tm, D_MODEL), BF16), pltpu.VMEM((2, SUBLANES + tm, CT_IN), F32)],
        compiler_params=pltpu.CompilerParams(dimension_semantics=("arbitrary", "arbitrary"),
                                             vmem_limit_bytes=VMEM_LIMIT),
        name="inproj",
    )(x, norm_g.reshape(1, D_MODEL), *weights, conv_w, sconv_w, alog, dtb)


def _gdn_kernel(q_ref, k_ref, v_ref, zs_ref, gbeta_ref, ng_ref, o_ref, state_ref):
    nb, ts = q_ref.shape[0], q_ref.shape[1]

    @pl.when(pl.program_id(1) == 0)
    def _():
        state_ref[...] = jnp.zeros_like(state_ref)

    ri = lax.broadcasted_iota(I32, (CHUNK, CHUNK), 0)
    ci = lax.broadcasted_iota(I32, (CHUNK, CHUNK), 1)
    causal = ri >= ci
    strict = ri > ci
    ltri = causal.astype(BF16)
    eye = (ri == ci).astype(F32)
    sh = INV_BLOCK.bit_length() - 1
    diag_blk = (ri >> sh) == (ci >> sh)
    off_blks = []
    while (1 << sh) < CHUNK:
        off_blks.append(((ri >> (sh + 1)) == (ci >> (sh + 1))) & ((ri >> sh) != (ci >> sh)) & strict)
        sh += 1
    ng = ng_ref[...]
    hs = range(nb * HEADS)
    bi = [u // HEADS for u in hs]
    hi = [u % HEADS for u in hs]

    def chunk(c, carry):
        rows = pl.ds(pl.multiple_of(c * CHUNK, CHUNK), CHUNK)
        cols = [slice(hi[h] * HEAD_DIM, (hi[h] + 1) * HEAD_DIM) for h in hs]
        gbc = [gbeta_ref[b, rows, :] for b in range(nb)]
        gcum = [_dot(ltri, gbc[b].astype(BF16)) for b in range(nb)]
        gcum_t = [gcum[b].T for b in range(nb)]
        gc = [gcum[bi[h]][:, hi[h]:hi[h] + 1] + gcum[bi[h]][:, 2 * HEADS + hi[h]:2 * HEADS + hi[h] + 1]
              for h in hs]
        gr = [gcum_t[bi[h]][hi[h]:hi[h] + 1, :] + gcum_t[bi[h]][2 * HEADS + hi[h]:2 * HEADS + hi[h] + 1, :]
              for h in hs]
        beta = [gbc[bi[h]][:, HEADS + hi[h]:HEADS + hi[h] + 1] for h in hs]
        g_last = [gc[h][CHUNK - 1:CHUNK, :] for h in hs]
        decay = [jnp.where(causal, jnp.exp(jnp.where(causal, gc[h] - gr[h], 0.0)), 0.0) for h in hs]
        eg = [jnp.exp(gc[h]) for h in hs]

        kf = [k_ref[bi[h], rows, cols[h]].astype(F32) for h in hs]
        kb = [kf[h] * beta[h] for h in hs]
        kt = [kf[h].T for h in hs]
        ktb = [kt[h].astype(BF16) for h in hs]
        lhs = [jnp.concatenate([kb[h].astype(BF16), q_ref[bi[h], rows, cols[h]]], axis=0) for h in hs]
        aq = [_dot(lhs[h], ktb[h]) for h in hs]
        aqk = [(aq[h][CHUNK:] * decay[h]).astype(BF16) for h in hs]
        lm = [jnp.where(strict, aq[h][:CHUNK] * decay[h], 0.0) for h in hs]
        m = [jnp.where(diag_blk, -lm[h], 0.0) for h in hs]
        tinv = [eye + m[h] for h in hs]
        mb = [m[h].astype(BF16) for h in hs]
        m = [_dot(mb[h], mb[h]) for h in hs]
        levels = INV_BLOCK.bit_length() - 1
        for j in range(1, levels):
            mb = [m[h].astype(BF16) for h in hs]
            if j < levels - 1:
                st = [_dot(jnp.concatenate([tinv[h].astype(BF16), mb[h]], axis=0), mb[h]) for h in hs]
                tinv = [tinv[h] + st[h][:CHUNK] for h in hs]
                m = [st[h][CHUNK:] for h in hs]
            else:
                tinv = [tinv[h] + _dot(tinv[h].astype(BF16), mb[h]) for h in hs]
        for off in off_blks:
            tb = [tinv[h].astype(BF16) for h in hs]
            ct = [_dot(jnp.where(off, lm[h], 0.0).astype(BF16), tb[h]) for h in hs]
            tinv = [tinv[h] - _dot(tb[h], ct[h].astype(BF16)) for h in hs]
        rhs = [jnp.concatenate([(v_ref[bi[h], rows, cols[h]].astype(F32) * beta[h]).astype(BF16),
                                (kb[h] * eg[h]).astype(BF16)], axis=1) for h in hs]
        uw = [_dot(tinv[h].astype(BF16), rhs[h]) for h in hs]

        sts = [state_ref[h] for h in hs]
        stb = [sts[h].astype(BF16) for h in hs]
        wq = [jnp.concatenate([uw[h][:, HEAD_DIM:].astype(BF16),
                               (q_ref[bi[h], rows, cols[h]].astype(F32) * eg[h]).astype(BF16)], axis=0) for h in hs]
        ws = [_dot(wq[h], stb[h]) for h in hs]
        vnb = [(uw[h][:, :HEAD_DIM] - ws[h][:CHUNK]).astype(BF16) for h in hs]
        kdt = [(kt[h] * jnp.exp(g_last[h] - gr[h])).astype(BF16) for h in hs]
        o = [ws[h][CHUNK:] + _dot(aqk[h], vnb[h]) for h in hs]
        for h in hs:
            state_ref[h] = sts[h] * jnp.exp(g_last[h]) + _dot(kdt[h], vnb[h])
        for h in hs:
            oh = _rms_norm(o[h], ng) * zs_ref[bi[h], rows, cols[h]].astype(F32)
            o_ref[bi[h], rows, cols[h]] = oh.astype(BF16)
        return carry

    lax.fori_loop(0, ts // CHUNK, chunk, 0)


def _gdn(q, k, v, zs, gbeta, norm_g):
    bsz, seq, _ = q.shape
    ts = min(TS_GDN, seq)
    nb = NB_GDN if bsz % NB_GDN == 0 else 1

    def tok(width):
        return pl.BlockSpec((nb, ts, width), lambda b, s: (b, s, 0))

    return pl.pallas_call(
        _gdn_kernel,
        grid=(bsz // nb, seq // ts),
        in_specs=[tok(GDN_WIDTH)] * 4 + [tok(LANES), pl.BlockSpec((1, HEAD_DIM), lambda b, s: (0, 0))],
        out_specs=tok(GDN_WIDTH),
        out_shape=jax.ShapeDtypeStruct((bsz, seq, GDN_WIDTH), BF16),
        scratch_shapes=[pltpu.VMEM((nb * HEADS, HEAD_DIM, HEAD_DIM), F32)],
        compiler_params=pltpu.CompilerParams(dimension_semantics=("arbitrary", "arbitrary"),
                                             vmem_limit_bytes=VMEM_LIMIT),
        name="gdn",
    )(q, k, v, zs, gbeta, norm_g.reshape(1, HEAD_DIM))


def _mix_kernel(og_ref, sb_ref, ga_ref, gt_ref, x_ref, wog_ref, wos_ref, wmix_ref, gffn_ref, wr_ref, br_ref,
                h1_ref, meta_ref, tw_ref, cnt_ref, xpad_ref,
                carry_ref, xt_ref, dvm_ref, dsm_ref, cvm_ref, csm_ref, zero_ref, sem_rows, sem_s, sem_z,
                *, nt, cap, bm):
    tm = x_ref.shape[0]
    i = pl.program_id(0)
    slot = i % ROW_BUFS

    def wait_rows(s):
        for _ in range(TOP_K):
            pltpu.make_async_copy(xt_ref.at[s], xpad_ref.at[pl.ds(0, tm * ROW_TILES), :], sem_rows.at[s]).wait()

    def slots_copy(par):
        return pltpu.make_async_copy(dvm_ref.at[par], dsm_ref.at[par], sem_s.at[par])

    def issue_rows(s, par):
        def issue(g, carry):
            for u in range(ISSUE_UNROLL):
                tok = g * (ISSUE_UNROLL // TOP_K) + u // TOP_K
                pltpu.make_async_copy(_row_tile(xt_ref.at[s], tok),
                                      _row_tile(xpad_ref, dsm_ref[par, u % TOP_K, tok]), sem_rows.at[s]).start()
            return carry

        lax.fori_loop(0, tm * TOP_K // ISSUE_UNROLL, issue, 0)

    @pl.when(i == 0)
    def _():
        carry_ref[...] = jnp.zeros_like(carry_ref)

    @pl.when(i >= ROW_BUFS)
    def _():
        wait_rows(slot)

    sub = tm // MIX_SPLIT
    rs = [slice(r * sub, (r + 1) * sub) for r in range(MIX_SPLIT)]
    ya = [_dot(og_ref[r, :], wog_ref[...]) for r in rs]
    yb = [_dot(sb_ref[r, :], wos_ref[...]) for r in rs]
    merged = [(ga_ref[r, :].astype(F32) * a + gt_ref[r, :].astype(F32) * b).astype(BF16)
              for r, a, b in zip(rs, ya, yb)]
    h1 = [x_ref[r, :] + _dot(mg, wmix_ref[...]) for r, mg in zip(rs, merged)]
    xn2 = [_rms_norm(h, gffn_ref[...]) for h in h1]
    x_hi = [v.astype(BF16) for v in xn2]
    x_lo = [(v - hi.astype(F32)).astype(BF16) for v, hi in zip(xn2, x_hi)]
    parts = [_dot(hi, wr_ref[...]) + _dot(lo, wr_ref[...]) for hi, lo in zip(x_hi, x_lo)]
    for r, h in zip(rs, h1):
        h1_ref[r, :] = h
    _store_row_tiles(xt_ref.at[slot], jnp.concatenate(xn2, axis=0))
    logits =jnp.concatenate([p[:, :LANES] + p[:, LANES:] for p in parts], axis=0) + br_ref[...]

    lane = lax.broadcasted_iota(I32, (tm, LANES), 1)
    lane_f = lane.astype(F32)
    neg_inf = jnp.float32(-jnp.inf)
    work = jnp.where(lane < N_EXPERTS, logits, neg_inf)
    hits, vals = [], []
    for _ in range(TOP_K):
        m = jnp.max(work, axis=-1, keepdims=True)
        first = jnp.min(jnp.where(work == m, lane_f, float(LANES)), axis=-1, keepdims=True)
        hit = lane_f == first
        work = jnp.where(hit, neg_inf, work)
        hits.append(hit)
        vals.append((m, first))
    exps = [jnp.exp(m - vals[0][0]) for m, _ in vals]
    denom = exps[0] + exps[1] + exps[2] + exps[3]

    sel = jnp.zeros((tm, LANES), F32)
    for hit in hits:
        sel = sel + hit.astype(F32)
    ri = lax.broadcasted_iota(I32, (tm, tm), 0)
    ci = lax.broadcasted_iota(I32, (tm, tm), 1)
    before = (ri > ci).astype(BF16)
    rank_mat = _dot(before, sel.astype(BF16)) + carry_ref[...]
    carry_ref[...] = carry_ref[...] + jnp.sum(sel, axis=0, keepdims=True)
    cnt_ref[...] = carry_ref[...]

    meta = jnp.zeros((tm, LANES), F32)
    rows = jnp.zeros((tm, LANES), F32)
    tw_out = jnp.zeros((tm, LANES), F32)
    for kk in range(TOP_K):
        rk = jnp.sum(jnp.where(hits[kk], rank_mat, 0.0), axis=-1, keepdims=True)
        meta = jnp.where(lane == kk, vals[kk][1], meta)
        meta = jnp.where(lane == TOP_K + kk, rk, meta)
        rows = jnp.where(lane == kk, vals[kk][1] * float(cap) + rk, rows)
        tw_out = jnp.where(lane == kk, exps[kk] / denom, tw_out)
    meta_ref[0] = meta.T[0:2 * TOP_K, :].astype(I32)
    tw_ref[...] = tw_out

    par = i % 2
    dvm_ref[par] = rows.T[0:SUBLANES, :].astype(I32)
    slots_copy(par).start()

    @pl.when(i >= 1)
    def _():
        slots_copy(1 - par).wait()
        issue_rows((i + ROW_BUFS - 1) % ROW_BUFS, 1 - par)

    @pl.when(i == nt - 1)
    def _():
        slots_copy(par).wait()
        issue_rows(slot, par)
        for back in range(min(ROW_BUFS, nt)):
            wait_rows((i + ROW_BUFS - back) % ROW_BUFS)
        cvm_ref[...] = jnp.broadcast_to(carry_ref[...], cvm_ref.shape).astype(I32)
        counts_to_smem = pltpu.make_async_copy(cvm_ref, csm_ref, sem_z)
        counts_to_smem.start()
        counts_to_smem.wait()
        zero_ref[...] = jnp.zeros_like(zero_ref)
        for e in range(N_EXPERTS):
            count = csm_ref[0, e]
            padded = (count + (bm - 1)) // bm * bm

            def zero_row(r, carry):
                pltpu.make_async_copy(zero_ref, _row_tile(xpad_ref, e * cap + r), sem_z).start()
                return carry

            def zero_wait(r, carry):
                pltpu.make_async_copy(zero_ref, _row_tile(xpad_ref, 0), sem_z).wait()
                return carry

            lax.fori_loop(count, padded, zero_row, 0)
            lax.fori_loop(count, padded, zero_wait, 0)


def _mix(og, sb, ga, gt, x, w_o_gdn, w_o_sconv, w_mix_out, norm_ffn_g, w_router, b_router, cap, bm):
    t = x.shape[0]
    tm = min(TM_MIX, t)
    nt = t // tm
    assert N_EXPERTS * cap < 2 ** 24, "row indices are formed in f32"
    wr = jnp.zeros((D_MODEL, LANES), F32).at[:, :N_EXPERTS].set(w_router)
    wr_hi = wr.astype(BF16)
    wr = jnp.concatenate([wr_hi, (wr - wr_hi.astype(F32)).astype(BF16)], axis=1)
    br = jnp.zeros((1, LANES), F32).at[0, :N_EXPERTS].set(b_router)

    def const(shape):
        return pl.BlockSpec(shape, lambda i: (0,) * len(shape), pipeline_mode=pl.Buffered(1))

    def tok(rows, width):
        return pl.BlockSpec((rows, width), lambda i: (i, 0))

    return pl.pallas_call(
        functools.partial(_mix_kernel, nt=nt, cap=cap, bm=bm),
        grid=(nt,),
        in_specs=[tok(tm, D_MODEL)] * 5 + [const((D_MODEL, D_MODEL))] * 3
        + [const((1, D_MODEL)), const((D_MODEL, 2 * LANES)), const((1, LANES))],
        out_specs=[tok(tm, D_MODEL), pl.BlockSpec((1, 2 * TOP_K, tm), lambda i: (i, 0, 0)), tok(tm, LANES),
                   pl.BlockSpec((1, LANES), lambda i: (0, 0)), pl.BlockSpec(memory_space=pl.ANY)],
        out_shape=[jax.ShapeDtypeStruct((t, D_MODEL), F32), jax.ShapeDtypeStruct((nt, 2 * TOP_K, tm), I32),
                   jax.ShapeDtypeStruct((t, LANES), F32), jax.ShapeDtypeStruct((1, LANES), F32),
                   jax.ShapeDtypeStruct((N_EXPERTS * cap * ROW_TILES, LANES), F32)],
        scratch_shapes=[pltpu.VMEM((1, LANES), F32), pltpu.VMEM((ROW_BUFS, tm * ROW_TILES, LANES), F32),
                        pltpu.VMEM((2, SUBLANES, tm), I32), pltpu.SMEM((2, SUBLANES, tm), I32),
                        pltpu.VMEM((SUBLANES, LANES), I32), pltpu.SMEM((SUBLANES, LANES), I32),
                        pltpu.VMEM((ROW_TILES, LANES), F32),
                        pltpu.SemaphoreType.DMA((ROW_BUFS,)), pltpu.SemaphoreType.DMA((2,)),
                        pltpu.SemaphoreType.DMA(())],
        compiler_params=pltpu.CompilerParams(dimension_semantics=("arbitrary",), vmem_limit_bytes=VMEM_LIMIT,
                                             has_side_effects=True, disable_bounds_checks=True),
        name="mix_router",
    )(og, sb, ga, gt, x, w_o_gdn.astype(BF16), w_o_sconv.astype(BF16), w_mix_out.astype(BF16),
      norm_ffn_g.reshape(1, D_MODEL), wr, br)


def _moe_kernel(be_ref, nu_ref, xr_ref, x_ref, wgu_ref, bgu_ref, wd_ref, bd_ref, y_ref, wgu_bf, wd_bf):
    bm = x_ref.shape[0] // ROW_TILES
    i = pl.program_id(0)

    @pl.when(i < nu_ref[0])
    def _():
        @pl.when((i == 0) | (be_ref[i] != be_ref[jnp.maximum(i - 1, 0)]))
        def _():
            def cast_rows(r, carry):
                rows = pl.ds(pl.multiple_of(r * CAST_ROWS, CAST_ROWS), CAST_ROWS)
                wgu_bf[rows, :] = wgu_ref[rows, :].astype(BF16)
                wd_bf[rows, :] = wd_ref[rows, :].astype(BF16)
                return carry

            lax.fori_loop(0, D_MODEL // CAST_ROWS, cast_rows, 0)

        xb = jnp.concatenate([_load_row_tiles(x_ref, bm, s).astype(BF16) for s in range(ROW_TILES)], axis=1)
        gu = _dot(xb, wgu_bf[...]) + bgu_ref[...]
        gate = jnp.minimum(gu[:, :D_FF], SWIGLU_LIMIT)
        up = jnp.clip(gu[:, D_FF:], -SWIGLU_LIMIT, SWIGLU_LIMIT)
        hid = (up + 1.0) * gate * _sigmoid(SWIGLU_ALPHA * gate)
        _store_row_tiles(y_ref, _dot(hid.astype(BF16), wd_bf[...]) + bd_ref[...])


def _moe(xpad, block_e, n_used, x_block, n_slots, w_gate_up, b_gate_up, w_down, b_down, bm):
    nb = n_slots // bm

    def rows(i, be, nu, xr):
        return (jnp.minimum(i, nu[0] - 1), 0)

    def x_rows(i, be, nu, xr):
        return (xr[i], 0)

    def expert(i, be, nu, xr):
        return (be[i], 0, 0)

    grid_spec = pltpu.PrefetchScalarGridSpec(
        num_scalar_prefetch=3,
        grid=(nb,),
        in_specs=[pl.BlockSpec((bm * ROW_TILES, LANES), x_rows),
                  pl.BlockSpec((None, D_MODEL, 2 * D_FF), expert),
                  pl.BlockSpec((None, 1, 2 * D_FF), expert),
                  pl.BlockSpec((None, D_FF, D_MODEL), expert),
                  pl.BlockSpec((None, 1, D_MODEL), expert)],
        out_specs=pl.BlockSpec((bm * ROW_TILES, LANES), rows),
        scratch_shapes=[pltpu.VMEM((D_MODEL, 2 * D_FF), BF16), pltpu.VMEM((D_FF, D_MODEL), BF16)],
    )
    return pl.pallas_call(
        _moe_kernel,
        grid_spec=grid_spec,
        out_shape=jax.ShapeDtypeStruct((n_slots * ROW_TILES, LANES), F32),
        compiler_params=pltpu.CompilerParams(dimension_semantics=("arbitrary",), vmem_limit_bytes=VMEM_LIMIT),
        name="moe_mlp",
    )(block_e, n_used, x_block, xpad, w_gate_up, b_gate_up.reshape(N_EXPERTS, 1, 2 * D_FF),
      w_down, b_down.reshape(N_EXPERTS, 1, D_MODEL))


def _combine_kernel(dest_ref, dest_next_ref, h1_ref, tw_ref, gf_ref, ypad_ref, out_ref, buf_ref, sem):
    tm = h1_ref.shape[0]
    i = pl.program_id(0)
    cur = i % 2

    def gather(d_ref, parity):
        def issue(g, carry):
            for u in range(ISSUE_UNROLL):
                tok = g * (ISSUE_UNROLL // TOP_K) + u // TOP_K
                pltpu.make_async_copy(_row_tile(ypad_ref, d_ref[0, 0, g * ISSUE_UNROLL + u]),
                                      _row_tile(buf_ref.at[parity * TOP_K + u % TOP_K], tok),
                                      sem.at[parity]).start()
            return carry

        lax.fori_loop(0, tm * TOP_K // ISSUE_UNROLL, issue, 0)

    @pl.when(i == 0)
    def _():
        gather(dest_ref, 0)

    @pl.when(i + 1 < pl.num_programs(0))
    def _():
        gather(dest_next_ref, 1 - cur)

    for kk in range(TOP_K):
        pltpu.make_async_copy(ypad_ref.at[pl.ds(0, tm * ROW_TILES), :], buf_ref.at[cur * TOP_K + kk],
                              sem.at[cur]).wait()

    for r0 in range(0, tm, COMB_ROWS):
        rr = slice(r0, r0 + COMB_ROWS)
        tw = tw_ref[rr, :]
        accs = []
        ssq = jnp.zeros((COMB_ROWS, 1), F32)
        for s in range(ROW_TILES):
            acc = h1_ref[rr, s * LANES:(s + 1) * LANES]
            for kk in range(TOP_K):
                acc = acc + tw[:, kk:kk + 1] * buf_ref[cur * TOP_K + kk,
                                                       pl.ds(r0 * ROW_TILES + s, COMB_ROWS, stride=ROW_TILES), :]
            ssq = ssq + jnp.sum(acc * acc, axis=-1, keepdims=True)
            accs.append(acc)
        inv = lax.rsqrt(ssq * (1.0 / D_MODEL) + NORM_EPS)
        for s in range(ROW_TILES):
            out_ref[rr, s * LANES:(s + 1) * LANES] = accs[s] * inv * gf_ref[:, s * LANES:(s + 1) * LANES]


def _combine(ypad, dest, h1, tw, norm_final_g):
    t = h1.shape[0]
    tm = min(TM_COMB, t)
    nt = t // tm
    dest3 = dest.reshape(nt, 1, tm * TOP_K)
    return pl.pallas_call(
        _combine_kernel,
        grid=(nt,),
        in_specs=[pl.BlockSpec((1, 1, tm * TOP_K), lambda i: (i, 0, 0), memory_space=pltpu.SMEM),
                  pl.BlockSpec((1, 1, tm * TOP_K), lambda i: (jnp.minimum(i + 1, nt - 1), 0, 0),
                               memory_space=pltpu.SMEM),
                  pl.BlockSpec((tm, D_MODEL), lambda i: (i, 0)),
                  pl.BlockSpec((tm, LANES), lambda i: (i, 0)),
                  pl.BlockSpec((1, D_MODEL), lambda i: (0, 0)),
                  pl.BlockSpec(memory_space=pl.ANY)],
        out_specs=pl.BlockSpec((tm, D_MODEL), lambda i: (i, 0)),
        out_shape=jax.ShapeDtypeStruct((t, D_MODEL), F32),
        scratch_shapes=[pltpu.VMEM((2 * TOP_K, tm * ROW_TILES, LANES), F32), pltpu.SemaphoreType.DMA((2,))],
        compiler_params=pltpu.CompilerParams(dimension_semantics=("arbitrary",), vmem_limit_bytes=VMEM_LIMIT,
                                             disable_bounds_checks=True),
        name="combine",
    )(dest3, dest3, h1, tw, norm_final_g.reshape(1, D_MODEL), ypad)


def _routing_tables(meta, cnt, t, bm, cap):
    counts = cnt[0, :N_EXPERTS].astype(I32)
    pcounts = ((counts + bm - 1) // bm) * bm
    pend = jnp.cumsum(pcounts)
    pstart = (pend - pcounts).astype(I32)
    idx, rank = meta[:, :TOP_K, :], meta[:, TOP_K:, :]
    onehot = idx[..., None] == jnp.arange(N_EXPERTS, dtype=I32)
    dest = jnp.sum(jnp.where(onehot, pstart, 0), axis=-1).astype(I32) + rank
    dest = dest.transpose(0, 2, 1).reshape(-1)
    n_slots = ((t * TOP_K + N_EXPERTS * (bm - 1) + bm - 1) // bm) * bm
    nb = n_slots // bm
    n_used = pend[-1] // bm
    starts = jnp.arange(nb, dtype=I32) * bm
    block_e = jnp.minimum(jnp.sum((pend[None, :] <= starts[:, None]).astype(I32), axis=1), N_EXPERTS - 1)
    x_block = block_e * (cap // bm) + (starts - pstart[block_e]) // bm
    last = jnp.maximum(n_used - 1, 0)
    used = jnp.arange(nb) < n_used
    block_e = jnp.where(used, block_e, block_e[last]).astype(I32)
    x_block = jnp.where(used, x_block, x_block[last]).astype(I32)
    return dest, block_e, n_used.reshape(1).astype(I32), x_block, n_slots


def kernel(x, norm_mix_g, w_in, gdn_conv_w, gdn_A_log, gdn_dt_bias, gdn_norm_g, w_o_gdn, sconv_w, w_o_sconv,
           w_mix_out, norm_ffn_g, w_router, b_router, w_gate_up, b_gate_up, w_down, b_down, norm_final_g):
    bsz, seq, _ = x.shape
    t = bsz * seq
    assert norm_mix_g.shape[0] == 1, "single-layer block only"
    q, k, v, zs, gbeta, sb, ga, gt = _inproj(x, norm_mix_g[0], w_in[0], gdn_conv_w[0], gdn_A_log[0],
                                             gdn_dt_bias[0], sconv_w[0])
    og = _gdn(q, k, v, zs, gbeta, gdn_norm_g[0])

    def flat(a):
        return a.reshape(t, a.shape[-1])

    bm = BM_MOE
    cap = pl.cdiv(t, bm) * bm
    h1, meta, tw, cnt, xpad = _mix(flat(og), flat(sb), flat(ga), flat(gt), flat(x), w_o_gdn[0], w_o_sconv[0],
                                   w_mix_out[0], norm_ffn_g[0], w_router[0], b_router[0], cap, bm)
    dest, block_e, n_used, x_block, n_slots = _routing_tables(meta, cnt, t, bm, cap)
    ypad = _moe(xpad, block_e, n_used, x_block, n_slots, w_gate_up[0], b_gate_up[0], w_down[0], b_down[0], bm)
    return _combine(ypad, dest, h1, tw, norm_final_g).reshape(bsz, seq, D_MODEL)
```
